```python
import math
import functools
import jax
import jax.numpy as jnp
from jax import lax
import numpy as np

D_MODEL = 2048
BATCH = 4
SEQ = 2048
DEPTH = 1
DEC_BATCH = 32
DEC_SEQ = 8
PAST_LEN = 16384
PAGE_SIZE = 128

N_HEADS = 8
HEAD_DIM = 64
V_DIM = 2 * HEAD_DIM
QK_WIDTH = N_HEADS * 2 * HEAD_DIM
ATTN_WIDTH = N_HEADS * V_DIM
D_CONV = D_MODEL // 2
CONV_WIDTH = 31
D_FF = 5632
FFN_CONV_WIDTH = 3
Q_BLOCK = 128
EPS = 1e-5
NEG_INF = -1e30
N_PAGES = PAST_LEN // PAGE_SIZE
SPLIT_POINTS = (2 * D_CONV,
                2 * D_CONV + QK_WIDTH,
                2 * D_CONV + 2 * QK_WIDTH,
                2 * D_CONV + 2 * QK_WIDTH + ATTN_WIDTH,
                2 * D_CONV + 2 * QK_WIDTH + ATTN_WIDTH + D_MODEL)
D_IN = 2 * D_CONV + 2 * QK_WIDTH + ATTN_WIDTH + 2 * D_MODEL

kernel_name = 'hybrid_conformer_diffattn_step'


def _rmsnorm(x, g):
    xf = x.astype(jnp.float32)
    y = xf * lax.rsqrt(jnp.mean(xf * xf, axis=-1, keepdims=True) + EPS)
    return (y * g.astype(jnp.float32)).astype(x.dtype)


def _layernorm(x, g, b):
    xf = x.astype(jnp.float32)
    xc = xf - jnp.mean(xf, axis=-1, keepdims=True)
    y = xc * lax.rsqrt(jnp.mean(xc * xc, axis=-1, keepdims=True) + EPS)
    return (y * g.astype(jnp.float32) + b.astype(jnp.float32)).astype(x.dtype)


def _causal_dwconv(x_ext, w, b):
    y = lax.conv_general_dilated(
        x_ext, w[:, None, :].astype(x_ext.dtype), window_strides=(1,), padding='VALID',
        dimension_numbers=('NWC', 'WIO', 'NWC'), feature_group_count=x_ext.shape[-1])
    return y + b.astype(y.dtype)


def _alibi_slopes():
    return jnp.exp2(-8.0 * jnp.arange(1, N_HEADS + 1, dtype=jnp.float32) / N_HEADS)


def _scores(q, k, qpos, kpos, slopes):
    s = jnp.einsum('bqhmd,bkhmd->bhmqk', q, k,
                   preferred_element_type=jnp.float32) * (HEAD_DIM ** -0.5)
    dist = (qpos[:, None] - kpos[None, :]).astype(jnp.float32)
    s = s - slopes[:, None, None, None] * dist
    return jnp.where(dist >= 0, s, NEG_INF)


def _diff_attn_prompt(q, k, v, slopes):
    B, T = q.shape[:2]
    n_blocks = T // Q_BLOCK
    q_blocks = jnp.moveaxis(q.reshape(B, n_blocks, Q_BLOCK, N_HEADS, 2, HEAD_DIM), 1, 0)
    kpos = jnp.arange(T, dtype=jnp.int32)
    vf = v.astype(jnp.float32)

    def one_block(args):
        qb, i = args
        qpos = i * Q_BLOCK + jnp.arange(Q_BLOCK, dtype=jnp.int32)
        p = jax.nn.softmax(_scores(qb, k, qpos, kpos, slopes), axis=-1)
        return jnp.einsum('bhmqk,bkhe->bqhme', p, vf)

    o = lax.map(one_block, (q_blocks, jnp.arange(n_blocks, dtype=jnp.int32)))
    return jnp.moveaxis(o, 0, 1).reshape(B, T, N_HEADS, 2, V_DIM)


def _diff_attn_sample(q, k, v, cache_k, cache_v, layer, page_table, slopes):
    qpos = PAST_LEN + jnp.arange(q.shape[1], dtype=jnp.int32)
    s = _scores(q, k, qpos, qpos, slopes)
    m = jnp.max(s, axis=-1)
    p = jnp.exp(s - m[..., None])
    l = jnp.sum(p, axis=-1)
    acc = jnp.einsum('bhmqk,bkhe->bhmqe', p, v.astype(jnp.float32))

    def page_step(carry, j):
        m, l, acc = carry
        phys = page_table[:, j]
        kp = cache_k[layer, phys]
        vp = cache_v[layer, phys]
        kpos = j * PAGE_SIZE + jnp.arange(PAGE_SIZE, dtype=jnp.int32)
        s = _scores(q, kp, qpos, kpos, slopes)
        m_new = jnp.maximum(m, jnp.max(s, axis=-1))
        corr = jnp.exp(m - m_new)
        p = jnp.exp(s - m_new[..., None])
        l = l * corr + jnp.sum(p, axis=-1)
        acc = acc * corr[..., None] + jnp.einsum('bhmqk,bkhe->bhmqe', p,
                                                 vp.astype(jnp.float32))
        return (m_new, l, acc), None

    (m, l, acc), _ = lax.scan(page_step, (m, l, acc),
                              jnp.arange(page_table.shape[1], dtype=jnp.int32))
    o = acc / l[..., None]
    return jnp.transpose(o, (0, 3, 1, 2, 4))


def _layer(x, conv_hist, ffn_hist, attend, lam, lam_init, norm_mix_g, w_in, conv_w, conv_b,
           conv_ln_g, conv_ln_b, w_conv_out, subln_g, w_attn_out, w_o, norm_ffn_g,
           w_ffn_in, ffn_conv_w, ffn_conv_b, w_ffn_out):
    B, T, _ = x.shape
    h = _rmsnorm(x, norm_mix_g)
    proj = h @ w_in
    glu_in, q, k, v, g_a, g_b = jnp.split(proj, SPLIT_POINTS, axis=-1)

    a_val, a_gate = jnp.split(glu_in, 2, axis=-1)
    u = a_val * jax.nn.sigmoid(a_gate)
    u_ext = jnp.concatenate([conv_hist.astype(u.dtype), u], axis=1)
    c = _causal_dwconv(u_ext, conv_w, conv_b)
    c = jax.nn.silu(_layernorm(c, conv_ln_g, conv_ln_b))
    out_a = c @ w_conv_out
    new_conv = u_ext[:, T:]

    q = q.reshape(B, T, N_HEADS, 2, HEAD_DIM)
    k = k.reshape(B, T, N_HEADS, 2, HEAD_DIM)
    v = v.reshape(B, T, N_HEADS, V_DIM)
    o = attend(q, k, v)
    od = o[..., 0, :] - lam * o[..., 1, :]
    od = _rmsnorm(od, subln_g) * (1.0 - lam_init)
    out_b = od.reshape(B, T, ATTN_WIDTH).astype(x.dtype) @ w_attn_out

    merged = jax.nn.sigmoid(g_a) * out_a + jax.nn.sigmoid(g_b) * out_b
    x = x + merged @ w_o

    h2 = _rmsnorm(x, norm_ffn_g)
    up_gate, up_val = jnp.split(h2 @ w_ffn_in, 2, axis=-1)
    g_ext = jnp.concatenate([ffn_hist.astype(up_gate.dtype), up_gate], axis=1)
    gc = _causal_dwconv(g_ext, ffn_conv_w, ffn_conv_b)
    x = x + (jax.nn.silu(gc) * up_val) @ w_ffn_out
    new_ffn = g_ext[:, T:]
    return x, k, v, new_conv, new_ffn


def setup_inputs(seed: int = 0) -> dict:
    key = jax.random.key(seed)
    ks = jax.random.split(key, 32)
    f32 = jnp.float32
    n_pool = (DEC_BATCH * N_PAGES * 5) // 4

    def nrm(k, shape, scale):
        return jax.random.normal(k, shape, f32) * scale

    def gain(k, shape):
        return 1.0 + 0.02 * jax.random.normal(k, shape, f32)

    inputs = {}
    inputs['x_prompt'] = nrm(ks[0], (BATCH, SEQ, D_MODEL), 1.0)
    inputs['x_sample'] = nrm(ks[1], (DEC_BATCH, DEC_SEQ, D_MODEL), 1.0)
    inputs['cache_k'] = nrm(ks[2], (DEPTH, n_pool, PAGE_SIZE, N_HEADS, 2, HEAD_DIM), 1.0)
    inputs['cache_v'] = nrm(ks[3], (DEPTH, n_pool, PAGE_SIZE, N_HEADS, V_DIM), 1.0)
    inputs['state_conv'] = nrm(ks[4], (DEPTH, DEC_BATCH, CONV_WIDTH - 1, D_CONV), 0.5)
    inputs['state_ffn'] = nrm(ks[5], (DEPTH, DEC_BATCH, FFN_CONV_WIDTH - 1, D_FF), 1.0)
    perm = jax.random.permutation(ks[6], n_pool)[:DEC_BATCH * N_PAGES]
    inputs['page_table'] = perm.reshape(DEC_BATCH, N_PAGES).astype(jnp.int32)
    inputs['norm_mix_g'] = gain(ks[7], (DEPTH, D_MODEL))
    inputs['w_in'] = nrm(ks[8], (DEPTH, D_MODEL, D_IN), D_MODEL ** -0.5)
    inputs['conv_w'] = nrm(ks[9], (DEPTH, CONV_WIDTH, D_CONV), CONV_WIDTH ** -0.5)
    inputs['conv_b'] = nrm(ks[10], (DEPTH, D_CONV), 0.02)
    inputs['conv_ln_g'] = gain(ks[11], (DEPTH, D_CONV))
    inputs['conv_ln_b'] = nrm(ks[12], (DEPTH, D_CONV), 0.02)
    inputs['w_conv_out'] = nrm(ks[13], (DEPTH, D_CONV, D_MODEL), D_CONV ** -0.5)
    inputs['lambda_q1'] = nrm(ks[14], (DEPTH, HEAD_DIM), 0.1)
    inputs['lambda_k1'] = nrm(ks[15], (DEPTH, HEAD_DIM), 0.1)
    inputs['lambda_q2'] = nrm(ks[16], (DEPTH, HEAD_DIM), 0.1)
    inputs['lambda_k2'] = nrm(ks[17], (DEPTH, HEAD_DIM), 0.1)
    inputs['subln_g'] = gain(ks[18], (DEPTH, V_DIM))
    inputs['w_attn_out'] = nrm(ks[19], (DEPTH, ATTN_WIDTH, D_MODEL), ATTN_WIDTH ** -0.5)
    inputs['w_o'] = nrm(ks[20], (DEPTH, D_MODEL, D_MODEL), D_MODEL ** -0.5)
    inputs['norm_ffn_g'] = gain(ks[21], (DEPTH, D_MODEL))
    inputs['w_ffn_in'] = nrm(ks[22], (DEPTH, D_MODEL, 2 * D_FF), D_MODEL ** -0.5)
    inputs['ffn_conv_w'] = nrm(ks[23], (DEPTH, FFN_CONV_WIDTH, D_FF), FFN_CONV_WIDTH ** -0.5)
    inputs['ffn_conv_b'] = nrm(ks[24], (DEPTH, D_FF), 0.02)
    inputs['w_ffn_out'] = nrm(ks[25], (DEPTH, D_FF, D_MODEL), D_FF ** -0.5)
    inputs['norm_final_g'] = gain(ks[26], (D_MODEL,))
    return inputs


def reference(x_prompt, x_sample, cache_k, cache_v, state_conv, state_ffn, page_table,
              norm_mix_g, w_in, conv_w, conv_b, conv_ln_g, conv_ln_b, w_conv_out,
              lambda_q1, lambda_k1, lambda_q2, lambda_k2, subln_g, w_attn_out, w_o,
              norm_ffn_g, w_ffn_in, ffn_conv_w, ffn_conv_b, w_ffn_out, norm_final_g):
    f32 = jnp.float32
    slopes = _alibi_slopes()
    xp, xs = x_prompt, x_sample
    k_p, v_p, c_p, f_p = [], [], [], []
    k_s, v_s, c_s, f_s = [], [], [], []
    for layer in range(DEPTH):
        lam_init = 0.8 - 0.6 * math.exp(-0.3 * layer)
        lam = (jnp.exp(jnp.sum(lambda_q1[layer].astype(f32) * lambda_k1[layer].astype(f32)))
               - jnp.exp(jnp.sum(lambda_q2[layer].astype(f32) * lambda_k2[layer].astype(f32)))
               + lam_init)
        weights = (norm_mix_g[layer], w_in[layer], conv_w[layer], conv_b[layer],
                   conv_ln_g[layer], conv_ln_b[layer], w_conv_out[layer], subln_g[layer],
                   w_attn_out[layer], w_o[layer], norm_ffn_g[layer], w_ffn_in[layer],
                   ffn_conv_w[layer], ffn_conv_b[layer], w_ffn_out[layer])
        zero_conv = jnp.zeros((xp.shape[0], CONV_WIDTH - 1, D_CONV), xp.dtype)
        zero_ffn = jnp.zeros((xp.shape[0], FFN_CONV_WIDTH - 1, D_FF), xp.dtype)
        attend_p = functools.partial(_diff_attn_prompt, slopes=slopes)
        xp, kp, vp, cp, fp = _layer(xp, zero_conv, zero_ffn, attend_p, lam, lam_init, *weights)
        attend_s = functools.partial(_diff_attn_sample, cache_k=cache_k, cache_v=cache_v,
                                     layer=layer, page_table=page_table, slopes=slopes)
        xs, ks_, vs_, cs_, fs_ = _layer(xs, state_conv[layer], state_ffn[layer], attend_s,
                                        lam, lam_init, *weights)
        k_p.append(kp); v_p.append(vp); c_p.append(cp); f_p.append(fp)
        k_s.append(ks_); v_s.append(vs_); c_s.append(cs_); f_s.append(fs_)
    y_prompt = _rmsnorm(xp, norm_final_g)
    y_sample = _rmsnorm(xs, norm_final_g)
    k_prompt = jnp.stack(k_p, axis=0)
    v_prompt = jnp.stack(v_p, axis=0)
    conv_prompt = jnp.stack(c_p, axis=0)
    ffn_prompt = jnp.stack(f_p, axis=0)
    k_sample = jnp.stack(k_s, axis=0)
    v_sample = jnp.stack(v_s, axis=0)
    conv_sample = jnp.stack(c_s, axis=0)
    ffn_sample = jnp.stack(f_s, axis=0)
    return (y_prompt, y_sample, k_prompt, v_prompt, conv_prompt, ffn_prompt,
            k_sample, v_sample, conv_sample, ffn_sample)
```

```python
import functools
import math

import jax
import jax.numpy as jnp
from jax import lax
from jax.experimental import pallas as pl
from jax.experimental.pallas import tpu as pltpu

D_MODEL = 2048
N_HEADS = 8
HEAD_DIM = 64
V_DIM = 2 * HEAD_DIM
D_CONV = D_MODEL // 2
CONV_WIDTH = 31
D_FF = 5632
FFN_CONV_WIDTH = 3
EPS = 1e-5
NEG_INF = -1e30
PAGE_SIZE = 128
LAM_INIT = 0.8 - 0.6 * math.exp(-0.3 * 0)

F32 = jnp.float32
BF16 = jnp.bfloat16

VMEM_LIMIT_BYTES = 56 * 1024 * 1024

PROJ_TN = 1024
PROJ_COL_TILES = 9
ATT_TQ = 256
PAGES_PER_STEP = 8
FFN_TF = 512
FFN_HALO = 16
CONV_HALO = 32


def _params(sem):
    return pltpu.CompilerParams(dimension_semantics=sem, vmem_limit_bytes=VMEM_LIMIT_BYTES)


def _sigmoid(x):
    return 1.0 / (1.0 + jnp.exp(-x))


def _rms(x, g):
    return x * lax.rsqrt(jnp.mean(x * x, axis=-1, keepdims=True) + EPS) * g


def _proj_kernel(x_ref, g_ref, w_ref, u_ref, q_ref, k_ref, v_ref, ga_ref, gb_ref, h_sc, val_sc):
    j = pl.program_id(1)

    @pl.when(j == 0)
    def _():
        h_sc[...] = _rms(x_ref[...], g_ref[...]).astype(BF16)

    acc = jnp.dot(h_sc[...], w_ref[...], preferred_element_type=F32)

    @pl.when(j == 0)
    def _():
        val_sc[...] = acc

    @pl.when(j == 1)
    def _():
        u_ref[...] = val_sc[...] * _sigmoid(acc)

    @pl.when(j == 2)
    def _():
        q_ref[...] = (acc * (HEAD_DIM ** -0.5)).astype(q_ref.dtype)

    @pl.when(j == 3)
    def _():
        k_ref[...] = acc

    @pl.when(j == 4)
    def _():
        v_ref[...] = acc

    @pl.when((j == 5) | (j == 6))
    def _():
        ga_ref[...] = _sigmoid(acc)

    @pl.when(j >= 7)
    def _():
        gb_ref[...] = _sigmoid(acc)


def _in_proj(x, g, w_bf, tm, q_dtype):
    t = x.shape[0]
    tn = PROJ_TN
    row = lambda i, j: (i, 0)
    return pl.pallas_call(
        _proj_kernel,
        grid=(t // tm, PROJ_COL_TILES),
        in_specs=[
            pl.BlockSpec((tm, D_MODEL), row),
            pl.BlockSpec((1, D_MODEL), lambda i, j: (0, 0)),
            pl.BlockSpec((D_MODEL, tn), lambda i, j: (0, j)),
        ],
        out_specs=[
            pl.BlockSpec((tm, tn), row),
            pl.BlockSpec((tm, tn), row),
            pl.BlockSpec((tm, tn), row),
            pl.BlockSpec((tm, tn), row),
            pl.BlockSpec((tm, tn), lambda i, j: (i, jnp.clip(j - 5, 0, 1))),
            pl.BlockSpec((tm, tn), lambda i, j: (i, jnp.clip(j - 7, 0, 1))),
        ],
        out_shape=[
            jax.ShapeDtypeStruct((t, D_CONV), F32),
            jax.ShapeDtypeStruct((t, N_HEADS * V_DIM), q_dtype),
            jax.ShapeDtypeStruct((t, N_HEADS * V_DIM), F32),
            jax.ShapeDtypeStruct((t, N_HEADS * V_DIM), F32),
            jax.ShapeDtypeStruct((t, D_MODEL), F32),
            jax.ShapeDtypeStruct((t, D_MODEL), F32),
        ],
        scratch_shapes=[pltpu.VMEM((tm, D_MODEL), BF16), pltpu.VMEM((tm, tn), F32)],
        compiler_params=_params(("arbitrary", "arbitrary")),
        name="in_proj",
    )(x, g, w_bf)


def _ln_swish(c, g, b):
    mu = jnp.mean(c, axis=-1, keepdims=True)
    cc = c - mu
    y = cc * lax.rsqrt(jnp.mean(cc * cc, axis=-1, keepdims=True) + EPS) * g + b
    return y * _sigmoid(y)


def _conv_prompt_kernel(tiles_per_seq, rows_chunk, u_ref, halo_ref, w_ref, b_ref, g_ref, beta_ref,
                        o_ref, ext_sc, c_sc):
    i = pl.program_id(0)
    tm = u_ref.shape[0]
    first = (i % tiles_per_seq) == 0
    ext_sc[0:CONV_HALO, :] = jnp.where(first, 0.0, halo_ref[...])
    ext_sc[CONV_HALO:, :] = u_ref[...]
    lead = CONV_HALO - (CONV_WIDTH - 1)
    col_chunk = 256

    def body(r, carry):
        r0 = pl.multiple_of(r * rows_chunk, rows_chunk)
        for c0 in range(0, D_CONV, col_chunk):
            acc = jnp.broadcast_to(b_ref[:, c0:c0 + col_chunk], (rows_chunk, col_chunk))
            win = ext_sc[pl.ds(r0, rows_chunk + CONV_HALO), c0:c0 + col_chunk]
            for tap in range(CONV_WIDTH):
                acc = acc + w_ref[tap:tap + 1, c0:c0 + col_chunk] * win[lead + tap:lead + tap + rows_chunk]
            c_sc[pl.ds(r0, rows_chunk), c0:c0 + col_chunk] = acc
        return carry

    lax.fori_loop(0, tm // rows_chunk, body, 0)
    o_ref[...] = _ln_swish(c_sc[...], g_ref[...], beta_ref[...]).astype(o_ref.dtype)


def _conv_prompt(u, w, b, g, beta, seq, tm):
    t = u.shape[0]
    tiles_per_seq = seq // tm
    halo_per_tile = tm // CONV_HALO
    vec = lambda i: (0, 0)
    return pl.pallas_call(
        functools.partial(_conv_prompt_kernel, tiles_per_seq, 32),
        grid=(t // tm,),
        in_specs=[
            pl.BlockSpec((tm, D_CONV), lambda i: (i, 0)),
            pl.BlockSpec((CONV_HALO, D_CONV), lambda i: (jnp.maximum(i * halo_per_tile - 1, 0), 0)),
            pl.BlockSpec((CONV_WIDTH, D_CONV), vec),
            pl.BlockSpec((1, D_CONV), vec),
            pl.BlockSpec((1, D_CONV), vec),
            pl.BlockSpec((1, D_CONV), vec),
        ],
        out_specs=pl.BlockSpec((tm, D_CONV), lambda i: (i, 0)),
        out_shape=jax.ShapeDtypeStruct((t, D_CONV), BF16),
        scratch_shapes=[pltpu.VMEM((tm + CONV_HALO, D_CONV), F32), pltpu.VMEM((tm, D_CONV), F32)],
        compiler_params=_params(("arbitrary",)),
        name="conv_prompt",
    )(u, u, w, b, g, beta)


def _conv_sample_kernel(u_ref, st_ref, w_ref, b_ref, g_ref, beta_ref, o_ref, new_ref, ext_sc):
    nb, dec, _ = u_ref.shape
    hist = CONV_WIDTH - 1
    ext_sc[:, 0:hist, :] = st_ref[...]
    ext_sc[:, hist:hist + dec, :] = u_ref[...]
    acc = jnp.broadcast_to(b_ref[...][None], (nb, dec, D_CONV))
    for tap in range(CONV_WIDTH):
        acc = acc + w_ref[tap:tap + 1, :][None] * ext_sc[:, tap:tap + dec, :]
    o_ref[...] = _ln_swish(acc, g_ref[...][None], beta_ref[...][None]).astype(o_ref.dtype)
    new_ref[...] = ext_sc[:, dec:dec + hist, :]


def _conv_sample(u3, state, w, b, g, beta):
    nb, dec, _ = u3.shape
    hist = CONV_WIDTH - 1
    return pl.pallas_call(
        _conv_sample_kernel,
        out_shape=[
            jax.ShapeDtypeStruct((nb, dec, D_CONV), BF16),
            jax.ShapeDtypeStruct((nb, hist, D_CONV), F32),
        ],
        scratch_shapes=[pltpu.VMEM((nb, hist + dec, D_CONV), F32)],
        compiler_params=pltpu.CompilerParams(vmem_limit_bytes=VMEM_LIMIT_BYTES),
        name="conv_sample",
    )(u3, state, w, b, g, beta)


def _lambda(lq1, lk1, lq2, lk2):
    return (jnp.exp(jnp.sum(lq1 * lk1, axis=-1, keepdims=True))
            - jnp.exp(jnp.sum(lq2 * lk2, axis=-1, keepdims=True)) + LAM_INIT)


def _stack_maps(q):
    first = (lax.broadcasted_iota(jnp.int32, q.shape, 1) % V_DIM) < HEAD_DIM
    zero = jnp.zeros_like(q)
    return jnp.concatenate([jnp.where(first, q, zero), jnp.where(first, zero, q)], axis=0)


def _online_update(s, v_bf, m_ref, l_ref, acc_ref):
    m_prev = m_ref[...]
    m_new = jnp.maximum(m_prev, jnp.max(s, axis=-1, keepdims=True))
    alpha = jnp.exp(m_prev - m_new)
    p = jnp.exp(s - m_new)
    l_ref[...] = alpha * l_ref[...] + jnp.sum(p, axis=-1, keepdims=True)
    acc_ref[...] = alpha * acc_ref[...] + jnp.dot(p.astype(BF16), v_bf, preferred_element_type=F32)
    m_ref[...] = m_new


def _diff_out(acc, l, rows, lam, g):
    o = acc / l
    od = o[0:rows] - lam * o[rows:2 * rows]
    return _rms(od, g) * (1.0 - LAM_INIT)


def _attn_prompt_kernel(slopes_ref, q_ref, k_ref, v_ref, lq1, lk1, lq2, lk2, g_ref, o_ref,
                        kb_sc, vb_sc, q2_sc, m_sc, l_sc, acc_sc):
    h = pl.program_id(1)
    qi = pl.program_id(2)
    tq = q_ref.shape[0]
    slope = slopes_ref[h]

    @pl.when(qi == 0)
    def _():
        kb_sc[...] = k_ref[...].astype(BF16)
        vb_sc[...] = v_ref[...].astype(BF16)

    q2_sc[...] = _stack_maps(q_ref[...])
    m_sc[...] = jnp.full_like(m_sc, -jnp.inf)
    l_sc[...] = jnp.zeros_like(l_sc)
    acc_sc[...] = jnp.zeros_like(acc_sc)

    row = lax.broadcasted_iota(jnp.int32, (2 * tq, tq), 0)
    col = lax.broadcasted_iota(jnp.int32, (2 * tq, tq), 1)
    rel = (jnp.where(row >= tq, row - tq, row) - col).astype(F32)

    def scores(kb):
        start = pl.multiple_of(kb * tq, tq)
        s = lax.dot_general(q2_sc[...], kb_sc[pl.ds(start, tq), :], (((1,), (1,)), ((), ())),
                            preferred_element_type=F32)
        dist = rel + ((qi - kb) * tq).astype(F32)
        return s - slope * dist, vb_sc[pl.ds(start, tq), :]

    def body(kb, carry):
        s, v_bf = scores(kb)
        _online_update(s, v_bf, m_sc, l_sc, acc_sc)
        return carry

    lax.fori_loop(0, qi, body, 0)
    s, v_bf = scores(qi)
    _online_update(jnp.where(rel >= 0.0, s, NEG_INF), v_bf, m_sc, l_sc, acc_sc)

    lam = _lambda(lq1[...], lk1[...], lq2[...], lk2[...])
    o_ref[...] = _diff_out(acc_sc[...], l_sc[...], tq, lam, g_ref[...]).astype(o_ref.dtype)


def _attn_prompt(q, k, v, slopes, lams, subln_g, batch, seq):
    tq = ATT_TQ
    nq = seq // tq
    vec = lambda b, h, i: (0, 0)
    kv_spec = pl.BlockSpec((seq, V_DIM), lambda b, h, i: (b, h))
    return pl.pallas_call(
        _attn_prompt_kernel,
        grid=(batch, N_HEADS, nq),
        in_specs=[
            pl.BlockSpec(memory_space=pltpu.SMEM),
            pl.BlockSpec((tq, V_DIM), lambda b, h, i: (b * nq + i, h)),
            kv_spec,
            kv_spec,
            pl.BlockSpec((1, HEAD_DIM), vec),
            pl.BlockSpec((1, HEAD_DIM), vec),
            pl.BlockSpec((1, HEAD_DIM), vec),
            pl.BlockSpec((1, HEAD_DIM), vec),
            pl.BlockSpec((1, V_DIM), vec),
        ],
        out_specs=pl.BlockSpec((tq, V_DIM), lambda b, h, i: (b * nq + i, h)),
        out_shape=jax.ShapeDtypeStruct((batch * seq, N_HEADS * V_DIM), BF16),
        scratch_shapes=[
            pltpu.VMEM((seq, V_DIM), BF16),
            pltpu.VMEM((seq, V_DIM), BF16),
            pltpu.VMEM((2 * tq, V_DIM), BF16),
            pltpu.VMEM((2 * tq, 1), F32),
            pltpu.VMEM((2 * tq, 1), F32),
            pltpu.VMEM((2 * tq, V_DIM), F32),
        ],
        compiler_params=_params(("arbitrary", "arbitrary", "arbitrary")),
        name="attn_prompt",
    )(slopes, q, k, v, *lams, subln_g)


def _attn_sample_kernel(past_len, pt_ref, q_ref, kn_ref, vn_ref, *rest):
    npg = PAGES_PER_STEP
    k_refs = rest[0:npg]
    v_refs = rest[npg:2 * npg]
    lq1, lk1, lq2, lk2, g_ref, o_ref, kb_sc, vb_sc, m_sc, l_sc, acc_sc = rest[2 * npg:]
    j = pl.program_id(1)
    nsteps = pl.num_programs(1)
    dec = q_ref.shape[1]
    keys = npg * PAGE_SIZE

    @pl.when(j == 0)
    def _():
        m_sc[...] = jnp.full_like(m_sc, -jnp.inf)
        l_sc[...] = jnp.zeros_like(l_sc)
        acc_sc[...] = jnp.zeros_like(acc_sc)

    for r in range(npg):
        kb_sc[r * PAGE_SIZE:(r + 1) * PAGE_SIZE, :] = k_refs[r][0].astype(BF16)
        vb_sc[r * PAGE_SIZE:(r + 1) * PAGE_SIZE, :] = v_refs[r][0].astype(BF16)

    q2 = _stack_maps(q_ref[0]).astype(BF16)
    qrow = lax.broadcasted_iota(jnp.int32, (2 * dec, keys), 0)
    qrow = jnp.where(qrow >= dec, qrow - dec, qrow)
    col = lax.broadcasted_iota(jnp.int32, (2 * dec, keys), 1)
    dist = (past_len + qrow - (j * keys + col)).astype(F32)

    for h in range(N_HEADS):
        cols = slice(h * V_DIM, (h + 1) * V_DIM)
        s = lax.dot_general(q2[:, cols], kb_sc[:, cols], (((1,), (1,)), ((), ())),
                            preferred_element_type=F32)
        s = s - (2.0 ** -(h + 1)) * dist
        _online_update(s, vb_sc[:, cols], m_sc.at[h], l_sc.at[h], acc_sc.at[h])

    @pl.when(j == nsteps - 1)
    def _():
        lam = _lambda(lq1[...], lk1[...], lq2[...], lk2[...])
        r2 = lax.broadcasted_iota(jnp.int32, (2 * dec, PAGE_SIZE), 0)
        r2 = jnp.where(r2 >= dec, r2 - dec, r2)
        c2 = lax.broadcasted_iota(jnp.int32, (2 * dec, PAGE_SIZE), 1)
        d2 = (r2 - c2).astype(F32)
        fill = jnp.zeros((PAGE_SIZE - dec, kn_ref.shape[2]), F32)
        k_new = jnp.concatenate([kn_ref[0], fill], axis=0).astype(BF16)
        v_new = jnp.concatenate([vn_ref[0], fill], axis=0).astype(BF16)
        for h in range(N_HEADS):
            cols = slice(h * V_DIM, (h + 1) * V_DIM)
            s = lax.dot_general(q2[:, cols], k_new[:, cols], (((1,), (1,)), ((), ())),
                                preferred_element_type=F32)
            s = jnp.where(d2 >= 0.0, s - (2.0 ** -(h + 1)) * d2, NEG_INF)
            _online_update(s, v_new[:, cols], m_sc.at[h], l_sc.at[h], acc_sc.at[h])
            o_ref[0, :, cols] = _diff_out(acc_sc[h], l_sc[h], dec, lam, g_ref[...]).astype(o_ref.dtype)


def _attn_sample(q3, k3, v3, cache_k, cache_v, page_table, lams, subln_g, past_len):
    nb, dec, width = q3.shape
    npg = PAGES_PER_STEP
    n_pages = page_table.shape[1]
    steps = n_pages // npg
    tok = pl.BlockSpec((1, dec, width), lambda b, j, pt: (b, 0, 0))
    vec = lambda b, j, pt: (0, 0)

    def page_spec(r):
        return pl.BlockSpec((1, PAGE_SIZE, width),
                            lambda b, j, pt: (pt[b * n_pages + j * npg + r], 0, 0))

    grid_spec = pltpu.PrefetchScalarGridSpec(
        num_scalar_prefetch=1,
        grid=(nb, steps),
        in_specs=[tok, tok, tok]
        + [page_spec(r) for r in range(npg)]
        + [page_spec(r) for r in range(npg)]
        + [pl.BlockSpec((1, HEAD_DIM), vec)] * 4
        + [pl.BlockSpec((1, V_DIM), vec)],
        out_specs=tok,
        scratch_shapes=[
            pltpu.VMEM((npg * PAGE_SIZE, width), BF16),
            pltpu.VMEM((npg * PAGE_SIZE, width), BF16),
            pltpu.VMEM((N_HEADS, 2 * dec, 1), F32),
            pltpu.VMEM((N_HEADS, 2 * dec, 1), F32),
            pltpu.VMEM((N_HEADS, 2 * dec, V_DIM), F32),
        ],
    )
    return pl.pallas_call(
        functools.partial(_attn_sample_kernel, past_len),
        grid_spec=grid_spec,
        out_shape=jax.ShapeDtypeStruct((nb, dec, width), F32),
        compiler_params=_params(("arbitrary", "arbitrary")),
        name="attn_sample",
    )(page_table.reshape(-1), q3, k3, v3, *([cache_k] * npg), *([cache_v] * npg), *lams, subln_g)


def _merge_kernel(c_ref, od_ref, ga_ref, gb_ref, x_ref, wco_ref, wao_ref, wo_ref, g_ref, x1_ref, h2_ref):
    out_a = jnp.dot(c_ref[...].astype(BF16), wco_ref[...], preferred_element_type=F32)
    out_b = jnp.dot(od_ref[...].astype(BF16), wao_ref[...], preferred_element_type=F32)
    merged = ga_ref[...] * out_a + gb_ref[...] * out_b
    x1 = x_ref[...] + jnp.dot(merged.astype(BF16), wo_ref[...], preferred_element_type=F32)
    x1_ref[...] = x1
    h2_ref[...] = _rms(x1, g_ref[...]).astype(BF16)


def _merge(c_act, od, sga, sgb, x, wco, wao, wo, g, tm):
    t = x.shape[0]
    row = lambda i: (i, 0)
    fixed = lambda i: (0, 0)
    resident = functools.partial(pl.BlockSpec, index_map=fixed, pipeline_mode=pl.Buffered(1))
    return pl.pallas_call(
        _merge_kernel,
        grid=(t // tm,),
        in_specs=[
            pl.BlockSpec((tm, D_CONV), row),
            pl.BlockSpec((tm, N_HEADS * V_DIM), row),
            pl.BlockSpec((tm, D_MODEL), row),
            pl.BlockSpec((tm, D_MODEL), row),
            pl.BlockSpec((tm, D_MODEL), row),
            resident((D_CONV, D_MODEL)),
            resident((N_HEADS * V_DIM, D_MODEL)),
            resident((D_MODEL, D_MODEL)),
            pl.BlockSpec((1, D_MODEL), fixed),
        ],
        out_specs=[pl.BlockSpec((tm, D_MODEL), row), pl.BlockSpec((tm, D_MODEL), row)],
        out_shape=[jax.ShapeDtypeStruct((t, D_MODEL), F32), jax.ShapeDtypeStruct((t, D_MODEL), BF16)],
        compiler_params=_params(("arbitrary",)),
        name="merge",
    )(c_act, od, sga, sgb, x, wco, wao, wo, g)


def _ffn_tail(f, gc, h2, wv_ref, wout_ref, x1_ref, gfin_ref, y_ref, acc_sc):
    val = jnp.dot(h2, wv_ref[...], preferred_element_type=F32)
    act = (gc * _sigmoid(gc) * val).astype(BF16)
    part = jnp.dot(act, wout_ref[...], preferred_element_type=F32)

    @pl.when(f == 0)
    def _():
        acc_sc[...] = part

    @pl.when(f > 0)
    def _():
        acc_sc[...] += part

    @pl.when(f == pl.num_programs(1) - 1)
    def _():
        y_ref[...] = _rms(x1_ref[...] + acc_sc[...], gfin_ref[...])


def _ffn_prompt_kernel(tiles_per_seq, h2_ref, halo_ref, wg_ref, wv_ref, cw_ref, cb_ref, wout_ref, x1_ref,
                       gfin_ref, y_ref, tail_ref, acc_sc, gext_sc):
    i = pl.program_id(0)
    f = pl.program_id(1)
    tm = h2_ref.shape[0]
    h2 = h2_ref[...]
    gate = jnp.dot(h2, wg_ref[...], preferred_element_type=F32)
    gate_halo = jnp.dot(halo_ref[...], wg_ref[...], preferred_element_type=F32)
    first = (i % tiles_per_seq) == 0
    gext_sc[0:FFN_HALO, :] = jnp.where(first, 0.0, gate_halo)
    gext_sc[FFN_HALO:, :] = gate
    tail_ref[0] = gate[tm - 8:tm, :]
    gc = (cw_ref[0:1, :] * gext_sc[pl.ds(FFN_HALO - 2, tm), :]
          + cw_ref[1:2, :] * gext_sc[pl.ds(FFN_HALO - 1, tm), :]
          + cw_ref[2:3, :] * gate + cb_ref[...])
    _ffn_tail(f, gc, h2, wv_ref, wout_ref, x1_ref, gfin_ref, y_ref, acc_sc)


def _ffn_sample_kernel(dec, h2_ref, e0_ref, e1_ref, wg_ref, wv_ref, cw_ref, cb_ref, wout_ref, x1_ref,
                       gfin_ref, y_ref, gate_ref, acc_sc):
    f = pl.program_id(1)
    h2 = h2_ref[...]
    gate = jnp.dot(h2, wg_ref[...], preferred_element_type=F32)
    gate_ref[...] = gate
    pos = lax.broadcasted_iota(jnp.int32, gate.shape, 0) % dec
    g1 = jnp.where(pos == 0, e1_ref[...], pltpu.roll(gate, 1, 0))
    g2 = jnp.where(pos <= 1, e0_ref[...], pltpu.roll(gate, 2, 0))
    gc = cw_ref[0:1, :] * g2 + cw_ref[1:2, :] * g1 + cw_ref[2:3, :] * gate + cb_ref[...]
    _ffn_tail(f, gc, h2, wv_ref, wout_ref, x1_ref, gfin_ref, y_ref, acc_sc)


def _ffn_specs(tm, tf):
    nf = D_FF // tf
    return dict(
        h2=pl.BlockSpec((tm, D_MODEL), lambda i, f: (i, 0)),
        wg=pl.BlockSpec((D_MODEL, tf), lambda i, f: (0, f)),
        wv=pl.BlockSpec((D_MODEL, tf), lambda i, f: (0, nf + f)),
        cw=pl.BlockSpec((FFN_CONV_WIDTH, tf), lambda i, f: (0, f)),
        cb=pl.BlockSpec((1, tf), lambda i, f: (0, f)),
        wout=pl.BlockSpec((tf, D_MODEL), lambda i, f: (f, 0)),
        x1=pl.BlockSpec((tm, D_MODEL), lambda i, f: (i, 0)),
        gfin=pl.BlockSpec((1, D_MODEL), lambda i, f: (0, 0)),
        y=pl.BlockSpec((tm, D_MODEL), lambda i, f: (i, 0)),
    )


def _ffn_prompt(h2, x1, w_in_bf, cw, cb, w_out_bf, gfin, seq, tm):
    t = h2.shape[0]
    tf = FFN_TF
    sp = _ffn_specs(tm, tf)
    halo_per_tile = tm // FFN_HALO
    return pl.pallas_call(
        functools.partial(_ffn_prompt_kernel, seq // tm),
        grid=(t // tm, D_FF // tf),
        in_specs=[
            sp["h2"],
            pl.BlockSpec((FFN_HALO, D_MODEL), lambda i, f: (jnp.maximum(i * halo_per_tile - 1, 0), 0)),
            sp["wg"], sp["wv"], sp["cw"], sp["cb"], sp["wout"], sp["x1"], sp["gfin"],
        ],
        out_specs=[sp["y"], pl.BlockSpec((1, 8, tf), lambda i, f: (i, 0, f))],
        out_shape=[jax.ShapeDtypeStruct((t, D_MODEL), F32), jax.ShapeDtypeStruct((t // tm, 8, D_FF), F32)],
        scratch_shapes=[pltpu.VMEM((tm, D_MODEL), F32), pltpu.VMEM((tm + FFN_HALO, tf), F32)],
        compiler_params=_params(("arbitrary", "arbitrary")),
        name="ffn_prompt",
    )(h2, h2, w_in_bf, w_in_bf, cw, cb, w_out_bf, x1, gfin)


def _ffn_sample(h2, x1, e0, e1, w_in_bf, cw, cb, w_out_bf, gfin, dec):
    t = h2.shape[0]
    tf = FFN_TF
    sp = _ffn_specs(t, tf)
    hist = pl.BlockSpec((t, tf), lambda i, f: (0, f))
    return pl.pallas_call(
        functools.partial(_ffn_sample_kernel, dec),
        grid=(1, D_FF // tf),
        in_specs=[sp["h2"], hist, hist, sp["wg"], sp["wv"], sp["cw"], sp["cb"], sp["wout"], sp["x1"],
                  sp["gfin"]],
        out_specs=[sp["y"], hist],
        out_shape=[jax.ShapeDtypeStruct((t, D_MODEL), F32), jax.ShapeDtypeStruct((t, D_FF), F32)],
        scratch_shapes=[pltpu.VMEM((t, D_MODEL), F32)],
        compiler_params=_params(("arbitrary", "arbitrary")),
        name="ffn_sample",
    )(h2, e0, e1, w_in_bf, w_in_bf, cw, cb, w_out_bf, x1, gfin)


def kernel(x_prompt, x_sample, cache_k, cache_v, state_conv, state_ffn, page_table, norm_mix_g, w_in, conv_w,
           conv_b, conv_ln_g, conv_ln_b, w_conv_out, lambda_q1, lambda_k1, lambda_q2, lambda_k2, subln_g,
           w_attn_out, w_o, norm_ffn_g, w_ffn_in, ffn_conv_w, ffn_conv_b, w_ffn_out, norm_final_g):
    batch, seq, _ = x_prompt.shape
    nb, dec, _ = x_sample.shape
    n_pool = cache_k.shape[1]
    past_len = page_table.shape[1] * PAGE_SIZE
    width = N_HEADS * V_DIM
    layer = 0

    row = lambda a: a[layer].reshape(1, -1)
    w_in_bf = w_in[layer].astype(BF16)
    wco_bf = w_conv_out[layer].astype(BF16)
    wao_bf = w_attn_out[layer].astype(BF16)
    wo_bf = w_o[layer].astype(BF16)
    wfi_bf = w_ffn_in[layer].astype(BF16)
    wfo_bf = w_ffn_out[layer].astype(BF16)
    g_mix, g_ffn, g_fin = row(norm_mix_g), row(norm_ffn_g), norm_final_g.reshape(1, -1)
    cw, cb, lng, lnb = conv_w[layer], row(conv_b), row(conv_ln_g), row(conv_ln_b)
    fcw, fcb = ffn_conv_w[layer], row(ffn_conv_b)
    lams = (row(lambda_q1), row(lambda_k1), row(lambda_q2), row(lambda_k2))
    sub_g = row(subln_g)
    slopes = jnp.asarray([2.0 ** -(h + 1) for h in range(N_HEADS)], F32)

    xp = x_prompt.reshape(batch * seq, D_MODEL)
    u_p, q_p, k_p, v_p, sga_p, sgb_p = _in_proj(xp, g_mix, w_in_bf, 512, BF16)
    c_p = _conv_prompt(u_p, cw, cb, lng, lnb, seq, 512)
    od_p = _attn_prompt(q_p, k_p, v_p, slopes, lams, sub_g, batch, seq)
    x1_p, h2_p = _merge(c_p, od_p, sga_p, sgb_p, xp, wco_bf, wao_bf, wo_bf, g_ffn, 256)
    ffn_tm = 512
    y_p, tail_p = _ffn_prompt(h2_p, x1_p, wfi_bf, fcw, fcb, wfo_bf, g_fin, seq, ffn_tm)

    xs = x_sample.reshape(nb * dec, D_MODEL)
    u_s, q_s, k_s, v_s, sga_s, sgb_s = _in_proj(xs, g_mix, w_in_bf, nb * dec, F32)
    c_s, conv_s = _conv_sample(u_s.reshape(nb, dec, D_CONV), state_conv[layer], cw, cb, lng, lnb)
    od_s = _attn_sample(q_s.reshape(nb, dec, width), k_s.reshape(nb, dec, width), v_s.reshape(nb, dec, width),
                        cache_k[layer].reshape(n_pool, PAGE_SIZE, width),
                        cache_v[layer].reshape(n_pool, PAGE_SIZE, width), page_table, lams, sub_g, past_len)
    x1_s, h2_s = _merge(c_s.reshape(nb * dec, D_CONV), od_s.reshape(nb * dec, width), sga_s, sgb_s, xs,
                        wco_bf, wao_bf, wo_bf, g_ffn, nb * dec)
    st = state_ffn[layer]
    pad = lambda a: jnp.pad(a, ((0, 0), (0, dec - a.shape[1]), (0, 0))).reshape(nb * dec, D_FF)
    e0 = pad(st)
    e1 = pad(st[:, 1:2])
    y_s, gate_s = _ffn_sample(h2_s, x1_s, e0, e1, wfi_bf, fcw, fcb, wfo_bf, g_fin, dec)

    tiles_per_seq = seq // ffn_tm
    ffn_p = tail_p.reshape(batch, tiles_per_seq, 8, D_FF)[:, -1, 8 - (FFN_CONV_WIDTH - 1):, :]
    return (
        y_p.reshape(batch, seq, D_MODEL),
        y_s.reshape(nb, dec, D_MODEL),
        k_p.reshape(1, batch, seq, N_HEADS, 2, HEAD_DIM),
        v_p.reshape(1, batch, seq, N_HEADS, V_DIM),
        u_p.reshape(batch, seq, D_CONV)[None, :, seq - (CONV_WIDTH - 1):, :],
        ffn_p[None],
        k_s.reshape(1, nb, dec, N_HEADS, 2, HEAD_DIM),
        v_s.reshape(1, nb, dec, N_HEADS, V_DIM),
        conv_s[None],
        gate_s.reshape(nb, dec, D_FF)[None, :, dec - (FFN_CONV_WIDTH - 1):, :],
    )
```

```python
import functools
import math

import jax
import jax.numpy as jnp
from jax import lax
from jax.experimental import pallas as pl
from jax.experimental.pallas import tpu as pltpu

D_MODEL = 2048
N_HEADS = 8
HEAD_DIM = 64
V_DIM = 2 * HEAD_DIM
D_CONV = D_MODEL // 2
CONV_WIDTH = 31
D_FF = 5632
FFN_CONV_WIDTH = 3
EPS = 1e-5
NEG_INF = -1e30
PAGE_SIZE = 128
LAM_INIT = 0.8 - 0.6 * math.exp(-0.3 * 0)

F32 = jnp.float32
BF16 = jnp.bfloat16

VMEM_LIMIT_BYTES = 56 * 1024 * 1024

PROJ_TN = 1024
PROJ_COL_TILES = 9
ATT_TQ = 256
PAGES_PER_STEP = 8
FFN_TF = 512
FFN_HALO = 16
CONV_HALO = 32


def _params(sem):
    return pltpu.CompilerParams(dimension_semantics=sem, vmem_limit_bytes=VMEM_LIMIT_BYTES)


def _sigmoid(x):
    return 1.0 / (1.0 + jnp.exp(-x))


def _rms(x, g):
    return x * lax.rsqrt(jnp.mean(x * x, axis=-1, keepdims=True) + EPS) * g


def _proj_kernel(x_ref, g_ref, w_ref, u_ref, q_ref, k_ref, v_ref, ga_ref, gb_ref, h_sc, val_sc):
    j = pl.program_id(1)

    @pl.when(j == 0)
    def _():
        h_sc[...] = _rms(x_ref[...], g_ref[...]).astype(BF16)

    acc = jnp.dot(h_sc[...], w_ref[...], preferred_element_type=F32)

    @pl.when(j == 0)
    def _():
        val_sc[...] = acc

    @pl.when(j == 1)
    def _():
        u_ref[...] = val_sc[...] * _sigmoid(acc)

    @pl.when(j == 2)
    def _():
        q_ref[...] = (acc * (HEAD_DIM ** -0.5)).astype(q_ref.dtype)

    @pl.when(j == 3)
    def _():
        k_ref[...] = acc

    @pl.when(j == 4)
    def _():
        v_ref[...] = acc

    @pl.when((j == 5) | (j == 6))
    def _():
        ga_ref[...] = _sigmoid(acc)

    @pl.when(j >= 7)
    def _():
        gb_ref[...] = _sigmoid(acc)


def _in_proj(x, g, w_bf, tm, q_dtype):
    t = x.shape[0]
    tn = PROJ_TN
    row = lambda i, j: (i, 0)
    return pl.pallas_call(
        _proj_kernel,
        grid=(t // tm, PROJ_COL_TILES),
        in_specs=[
            pl.BlockSpec((tm, D_MODEL), row),
            pl.BlockSpec((1, D_MODEL), lambda i, j: (0, 0)),
            pl.BlockSpec((D_MODEL, tn), lambda i, j: (0, j)),
        ],
        out_specs=[
            pl.BlockSpec((tm, tn), row),
            pl.BlockSpec((tm, tn), row),
            pl.BlockSpec((tm, tn), row),
            pl.BlockSpec((tm, tn), row),
            pl.BlockSpec((tm, tn), lambda i, j: (i, jnp.clip(j - 5, 0, 1))),
            pl.BlockSpec((tm, tn), lambda i, j: (i, jnp.clip(j - 7, 0, 1))),
        ],
        out_shape=[
            jax.ShapeDtypeStruct((t, D_CONV), F32),
            jax.ShapeDtypeStruct((t, N_HEADS * V_DIM), q_dtype),
            jax.ShapeDtypeStruct((t, N_HEADS * V_DIM), F32),
            jax.ShapeDtypeStruct((t, N_HEADS * V_DIM), F32),
            jax.ShapeDtypeStruct((t, D_MODEL), F32),
            jax.ShapeDtypeStruct((t, D_MODEL), F32),
        ],
        scratch_shapes=[pltpu.VMEM((tm, D_MODEL), BF16), pltpu.VMEM((tm, tn), F32)],
        compiler_params=_params(("arbitrary", "arbitrary")),
        name="in_proj",
    )(x, g, w_bf)


def _ln_swish(c, g, b):
    mu = jnp.mean(c, axis=-1, keepdims=True)
    cc = c - mu
    y = cc * lax.rsqrt(jnp.mean(cc * cc, axis=-1, keepdims=True) + EPS) * g + b
    return y * _sigmoid(y)


def _conv_prompt_kernel(tiles_per_seq, rows_chunk, u_ref, halo_ref, w_ref, b_ref, g_ref, beta_ref,
                        o_ref, ext_sc, c_sc):
    i = pl.program_id(0)
    tm = u_ref.shape[0]
    first = (i % tiles_per_seq) == 0
    ext_sc[0:CONV_HALO, :] = jnp.where(first, 0.0, halo_ref[...])
    ext_sc[CONV_HALO:, :] = u_ref[...]
    lead = CONV_HALO - (CONV_WIDTH - 1)
    col_chunk = 256

    def body(r, carry):
        r0 = pl.multiple_of(r * rows_chunk, rows_chunk)
        for c0 in range(0, D_CONV, col_chunk):
            acc = jnp.broadcast_to(b_ref[:, c0:c0 + col_chunk], (rows_chunk, col_chunk))
            win = ext_sc[pl.ds(r0, rows_chunk + CONV_HALO), c0:c0 + col_chunk]
            for tap in range(CONV_WIDTH):
                acc = acc + w_ref[tap:tap + 1, c0:c0 + col_chunk] * win[lead + tap:lead + tap + rows_chunk]
            c_sc[pl.ds(r0, rows_chunk), c0:c0 + col_chunk] = acc
        return carry

    lax.fori_loop(0, tm // rows_chunk, body, 0)
    o_ref[...] = _ln_swish(c_sc[...], g_ref[...], beta_ref[...]).astype(o_ref.dtype)


def _conv_prompt(u, w, b, g, beta, seq, tm):
    t = u.shape[0]
    tiles_per_seq = seq // tm
    halo_per_tile = tm // CONV_HALO
    vec = lambda i: (0, 0)
    return pl.pallas_call(
        functools.partial(_conv_prompt_kernel, tiles_per_seq, 32),
        grid=(t // tm,),
        in_specs=[
            pl.BlockSpec((tm, D_CONV), lambda i: (i, 0)),
            pl.BlockSpec((CONV_HALO, D_CONV), lambda i: (jnp.maximum(i * halo_per_tile - 1, 0), 0)),
            pl.BlockSpec((CONV_WIDTH, D_CONV), vec),
            pl.BlockSpec((1, D_CONV), vec),
            pl.BlockSpec((1, D_CONV), vec),
            pl.BlockSpec((1, D_CONV), vec),
        ],
        out_specs=pl.BlockSpec((tm, D_CONV), lambda i: (i, 0)),
        out_shape=jax.ShapeDtypeStruct((t, D_CONV), BF16),
        scratch_shapes=[pltpu.VMEM((tm + CONV_HALO, D_CONV), F32), pltpu.VMEM((tm, D_CONV), F32)],
        compiler_params=_params(("arbitrary",)),
        name="conv_prompt",
    )(u, u, w, b, g, beta)


def _conv_sample_kernel(u_ref, st_ref, w_ref, b_ref, g_ref, beta_ref, o_ref, new_ref, ext_sc):
    nb, dec, _ = u_ref.shape
    hist = CONV_WIDTH - 1
    ext_sc[:, 0:hist, :] = st_ref[...]
    ext_sc[:, hist:hist + dec, :] = u_ref[...]
    acc = jnp.broadcast_to(b_ref[...][None], (nb, dec, D_CONV))
    for tap in range(CONV_WIDTH):
        acc = acc + w_ref[tap:tap + 1, :][None] * ext_sc[:, tap:tap + dec, :]
    o_ref[...] = _ln_swish(acc, g_ref[...][None], beta_ref[...][None]).astype(o_ref.dtype)
    new_ref[...] = ext_sc[:, dec:dec + hist, :]


def _conv_sample(u3, state, w, b, g, beta):
    nb, dec, _ = u3.shape
    hist = CONV_WIDTH - 1
    return pl.pallas_call(
        _conv_sample_kernel,
        out_shape=[
            jax.ShapeDtypeStruct((nb, dec, D_CONV), BF16),
            jax.ShapeDtypeStruct((nb, hist, D_CONV), F32),
        ],
        scratch_shapes=[pltpu.VMEM((nb, hist + dec, D_CONV), F32)],
        compiler_params=pltpu.CompilerParams(vmem_limit_bytes=VMEM_LIMIT_BYTES),
        name="conv_sample",
    )(u3, state, w, b, g, beta)


NT_DIMS = (((1,), (1,)), ((), ()))


def _lambda(lq1, lk1, lq2, lk2):
    return (jnp.exp(jnp.sum(lq1 * lk1, axis=-1, keepdims=True))
            - jnp.exp(jnp.sum(lq2 * lk2, axis=-1, keepdims=True)) + LAM_INIT)


def _stack_maps(q):
    first = lax.broadcasted_iota(jnp.int32, q.shape, 1) < HEAD_DIM
    zero = jnp.zeros_like(q)
    return jnp.concatenate([jnp.where(first, q, zero), jnp.where(first, zero, q)], axis=0)


def _attn_prompt_kernel(slopes_ref, q_ref, k_ref, v_ref, lq1, lk1, lq2, lk2, gcol_ref, o_ref,
                        kb_sc, vt_sc, q2_sc, bias_sc, bdiag_sc, m_sc, l_sc, acc_sc):
    h = pl.program_id(1)
    qi = pl.program_id(2)
    tq = q_ref.shape[0]
    slope = slopes_ref[h]

    @pl.when(qi == 0)
    def _():
        kb_sc[...] = k_ref[...].astype(BF16)
        for kb in range(k_ref.shape[0] // tq):
            vt_sc[kb] = v_ref[kb * tq:(kb + 1) * tq, :].T.astype(BF16)
        key = lax.broadcasted_iota(jnp.int32, (tq, 2 * tq), 0)
        qry = lax.broadcasted_iota(jnp.int32, (tq, 2 * tq), 1)
        dist = (jnp.where(qry >= tq, qry - tq, qry) - key).astype(F32)
        bias_sc[...] = slope * dist
        bdiag_sc[...] = jnp.where(dist >= 0.0, slope * dist, -NEG_INF)

    q2_sc[...] = _stack_maps(q_ref[...])
    m_sc[...] = jnp.full_like(m_sc, -jnp.inf)
    l_sc[...] = jnp.zeros_like(l_sc)
    acc_sc[...] = jnp.zeros_like(acc_sc)

    def block(kb, bias_ref, beta):
        start = pl.multiple_of(kb * tq, tq)
        s = lax.dot_general(kb_sc[pl.ds(start, tq), :], q2_sc[...], NT_DIMS,
                            preferred_element_type=F32) - bias_ref[...]
        m_prev = m_sc[...]
        m_new = jnp.maximum(m_prev, jnp.max(s, axis=0, keepdims=True) - beta)
        alpha = jnp.exp(m_prev - m_new)
        p = jnp.exp(s - (m_new + beta))
        l_sc[...] = alpha * l_sc[...] + jnp.sum(p, axis=0, keepdims=True)
        acc_sc[...] = alpha * acc_sc[...] + jnp.dot(vt_sc[kb], p.astype(BF16), preferred_element_type=F32)
        m_sc[...] = m_new

    def body(kb, carry):
        block(kb, bias_sc, slope * ((qi - kb) * tq).astype(F32))
        return carry

    lax.fori_loop(0, qi, body, 0)
    block(qi, bdiag_sc, 0.0)

    lam = _lambda(lq1[...], lk1[...], lq2[...], lk2[...])
    o = acc_sc[...] / l_sc[...]
    od = o[:, 0:tq] - lam * o[:, tq:2 * tq]
    od = od * lax.rsqrt(jnp.mean(od * od, axis=0, keepdims=True) + EPS) * gcol_ref[...] * (1.0 - LAM_INIT)
    o_ref[...] = od.T.astype(o_ref.dtype)


def _attn_prompt(q, k, v, slopes, lams, subln_g, batch, seq):
    tq = ATT_TQ
    nq = seq // tq
    vec = lambda b, h, i: (0, 0)
    kv_spec = pl.BlockSpec((seq, V_DIM), lambda b, h, i: (b, h))
    return pl.pallas_call(
        _attn_prompt_kernel,
        grid=(batch, N_HEADS, nq),
        in_specs=[
            pl.BlockSpec(memory_space=pltpu.SMEM),
            pl.BlockSpec((tq, V_DIM), lambda b, h, i: (b * nq + i, h)),
            kv_spec,
            kv_spec,
            pl.BlockSpec((1, HEAD_DIM), vec),
            pl.BlockSpec((1, HEAD_DIM), vec),
            pl.BlockSpec((1, HEAD_DIM), vec),
            pl.BlockSpec((1, HEAD_DIM), vec),
            pl.BlockSpec((V_DIM, 1), vec),
        ],
        out_specs=pl.BlockSpec((tq, V_DIM), lambda b, h, i: (b * nq + i, h)),
        out_shape=jax.ShapeDtypeStruct((batch * seq, N_HEADS * V_DIM), BF16),
        scratch_shapes=[
            pltpu.VMEM((seq, V_DIM), BF16),
            pltpu.VMEM((nq, V_DIM, tq), BF16),
            pltpu.VMEM((2 * tq, V_DIM), BF16),
            pltpu.VMEM((tq, 2 * tq), F32),
            pltpu.VMEM((tq, 2 * tq), F32),
            pltpu.VMEM((1, 2 * tq), F32),
            pltpu.VMEM((1, 2 * tq), F32),
            pltpu.VMEM((V_DIM, 2 * tq), F32),
        ],
        compiler_params=_params(("arbitrary", "arbitrary", "arbitrary")),
        name="attn_prompt",
    )(slopes, q, k, v, *lams, subln_g.reshape(V_DIM, 1))


def _attn_sample_kernel(past_len, pt_ref, q_ref, kn_ref, vn_ref, *rest):
    npg = PAGES_PER_STEP
    k_refs = rest[0:npg]
    v_refs = rest[npg:2 * npg]
    lq1, lk1, lq2, lk2, g_ref, o_ref, kt_sc, vb_sc, qbd_sc, sb_sc, m_sc, l_sc, acc_sc = rest[2 * npg:]
    j = pl.program_id(1)
    dec = q_ref.shape[1]
    width = q_ref.shape[2]
    rows = 2 * N_HEADS * dec
    rows_per_head = 2 * dec
    keys = npg * PAGE_SIZE
    chunk = 256
    rows_per_chunk = (chunk // HEAD_DIM) * dec

    row1 = lax.broadcasted_iota(jnp.int32, (rows, 1), 0)
    slope = jnp.zeros((rows, 1), F32)
    for h in range(N_HEADS):
        slope = jnp.where(row1 // rows_per_head == h, 2.0 ** -(h + 1), slope)

    @pl.when(j == 0)
    def _():
        m_sc[...] = jnp.full_like(m_sc, -jnp.inf)
        l_sc[...] = jnp.zeros_like(l_sc)
        acc_sc[...] = jnp.zeros_like(acc_sc)
        qt = jnp.concatenate([q_ref[0]] * (rows // dec), axis=0)
        rr = lax.broadcasted_iota(jnp.int32, (rows, width), 0)
        cc = lax.broadcasted_iota(jnp.int32, (rows, width), 1)
        qbd_sc[...] = jnp.where(cc // HEAD_DIM == rr // dec, qt, 0.0).astype(BF16)
        qpos = lax.broadcasted_iota(jnp.int32, (rows, keys), 0) % dec
        col = lax.broadcasted_iota(jnp.int32, (rows, keys), 1)
        sb_sc[...] = slope * (past_len + qpos - col).astype(F32)

    for r in range(npg):
        kt_sc[:, r * PAGE_SIZE:(r + 1) * PAGE_SIZE] = k_refs[r][0].astype(BF16)
        for h in range(N_HEADS):
            vb_sc[h, r * PAGE_SIZE:(r + 1) * PAGE_SIZE, :] = v_refs[r][
                0, pl.ds(h, PAGE_SIZE, stride=N_HEADS), :].astype(BF16)

    gamma = slope * (j * keys).astype(F32)

    def update(r0, nrows, s, g, pv_fn):
        sl = slice(r0, r0 + nrows)
        m_prev = m_sc[sl]
        m_new = jnp.maximum(m_prev, jnp.max(s, axis=-1, keepdims=True) + g)
        alpha = jnp.exp(m_prev - m_new)
        p = jnp.exp(s - (m_new - g))
        l_sc[sl] = alpha * l_sc[sl] + jnp.sum(p, axis=-1, keepdims=True)
        acc_sc[sl] = alpha * acc_sc[sl] + pv_fn(p.astype(BF16))
        m_sc[sl] = m_new

    for c in range(width // chunk):
        r0 = c * rows_per_chunk
        s = jnp.dot(qbd_sc[r0:r0 + rows_per_chunk, c * chunk:(c + 1) * chunk],
                    kt_sc[c * chunk:(c + 1) * chunk, :], preferred_element_type=F32)
        s = s - sb_sc[r0:r0 + rows_per_chunk, :]
        h0 = r0 // rows_per_head

        def pv(pb, h0=h0):
            return jnp.concatenate(
                [jnp.dot(pb[i * rows_per_head:(i + 1) * rows_per_head], vb_sc[h0 + i], preferred_element_type=F32)
                 for i in range(rows_per_chunk // rows_per_head)], axis=0)

        update(r0, rows_per_chunk, s, gamma[r0:r0 + rows_per_chunk], pv)

    @pl.when(j == pl.num_programs(1) - 1)
    def _():
        fill = jnp.zeros((PAGE_SIZE - dec, width), F32)
        k_new = jnp.concatenate([kn_ref[0], fill], axis=0).astype(BF16)
        v_new = jnp.concatenate([vn_ref[0], fill], axis=0).astype(BF16)
        qpos = lax.broadcasted_iota(jnp.int32, (rows, PAGE_SIZE), 0) % dec
        col = lax.broadcasted_iota(jnp.int32, (rows, PAGE_SIZE), 1)
        dist = (qpos - col).astype(F32)
        s = lax.dot_general(qbd_sc[...], k_new, NT_DIMS, preferred_element_type=F32)
        s = jnp.where(dist >= 0.0, s - slope * dist, NEG_INF)

        def pv(pb):
            return jnp.concatenate(
                [jnp.dot(pb[h * rows_per_head:(h + 1) * rows_per_head], v_new[:, h * V_DIM:(h + 1) * V_DIM],
                         preferred_element_type=F32) for h in range(N_HEADS)], axis=0)

        update(0, rows, s, 0.0, pv)

        lam = _lambda(lq1[...], lk1[...], lq2[...], lk2[...])
        o = acc_sc[...] / l_sc[...]
        for h in range(N_HEADS):
            r0 = h * rows_per_head
            od = o[r0:r0 + dec] - lam * o[r0 + dec:r0 + 2 * dec]
            o_ref[0, :, h * V_DIM:(h + 1) * V_DIM] = (_rms(od, g_ref[...]) * (1.0 - LAM_INIT)).astype(o_ref.dtype)


def _attn_sample(q3, k3, v3, cache_kt, cache_vf, page_table, lams, subln_g, past_len):
    nb, dec, width = q3.shape
    npg = PAGES_PER_STEP
    n_pages = page_table.shape[1]
    rows = 2 * N_HEADS * dec
    keys = npg * PAGE_SIZE
    tok = pl.BlockSpec((1, dec, width), lambda b, j, pt: (b, 0, 0))
    vec = lambda b, j, pt: (0, 0)

    def page_spec(r, shape):
        return pl.BlockSpec((1,) + shape, lambda b, j, pt: (pt[b * n_pages + j * npg + r], 0, 0))

    grid_spec = pltpu.PrefetchScalarGridSpec(
        num_scalar_prefetch=1,
        grid=(nb, n_pages // npg),
        in_specs=[tok, tok, tok]
        + [page_spec(r, cache_kt.shape[1:]) for r in range(npg)]
        + [page_spec(r, cache_vf.shape[1:]) for r in range(npg)]
        + [pl.BlockSpec((1, HEAD_DIM), vec)] * 4
        + [pl.BlockSpec((1, V_DIM), vec)],
        out_specs=tok,
        scratch_shapes=[
            pltpu.VMEM((width, keys), BF16),
            pltpu.VMEM((N_HEADS, keys, V_DIM), BF16),
            pltpu.VMEM((rows, width), BF16),
            pltpu.VMEM((rows, keys), F32),
            pltpu.VMEM((rows, 1), F32),
            pltpu.VMEM((rows, 1), F32),
            pltpu.VMEM((rows, V_DIM), F32),
        ],
    )
    return pl.pallas_call(
        functools.partial(_attn_sample_kernel, past_len),
        grid_spec=grid_spec,
        out_shape=jax.ShapeDtypeStruct((nb, dec, width), F32),
        compiler_params=_params(("arbitrary", "arbitrary")),
        name="attn_sample",
    )(page_table.reshape(-1), q3, k3, v3, *([cache_kt] * npg), *([cache_vf] * npg), *lams, subln_g)


def _merge_kernel(c_ref, od_ref, ga_ref, gb_ref, x_ref, wco_ref, wao_ref, wo_ref, g_ref, x1_ref, h2_ref):
    out_a = jnp.dot(c_ref[...].astype(BF16), wco_ref[...], preferred_element_type=F32)
    out_b = jnp.dot(od_ref[...].astype(BF16), wao_ref[...], preferred_element_type=F32)
    merged = ga_ref[...] * out_a + gb_ref[...] * out_b
    x1 = x_ref[...] + jnp.dot(merged.astype(BF16), wo_ref[...], preferred_element_type=F32)
    x1_ref[...] = x1
    h2_ref[...] = _rms(x1, g_ref[...]).astype(BF16)


def _merge(c_act, od, sga, sgb, x, wco, wao, wo, g, tm):
    t = x.shape[0]
    row = lambda i: (i, 0)
    fixed = lambda i: (0, 0)
    resident = functools.partial(pl.BlockSpec, index_map=fixed, pipeline_mode=pl.Buffered(1))
    return pl.pallas_call(
        _merge_kernel,
        grid=(t // tm,),
        in_specs=[
            pl.BlockSpec((tm, D_CONV), row),
            pl.BlockSpec((tm, N_HEADS * V_DIM), row),
            pl.BlockSpec((tm, D_MODEL), row),
            pl.BlockSpec((tm, D_MODEL), row),
            pl.BlockSpec((tm, D_MODEL), row),
            resident((D_CONV, D_MODEL)),
            resident((N_HEADS * V_DIM, D_MODEL)),
            resident((D_MODEL, D_MODEL)),
            pl.BlockSpec((1, D_MODEL), fixed),
        ],
        out_specs=[pl.BlockSpec((tm, D_MODEL), row), pl.BlockSpec((tm, D_MODEL), row)],
        out_shape=[jax.ShapeDtypeStruct((t, D_MODEL), F32), jax.ShapeDtypeStruct((t, D_MODEL), BF16)],
        compiler_params=_params(("arbitrary",)),
        name="merge",
    )(c_act, od, sga, sgb, x, wco, wao, wo, g)


def _ffn_tail(f, gc, h2, wv_ref, wout_ref, x1_ref, gfin_ref, y_ref, acc_sc):
    val = jnp.dot(h2, wv_ref[...], preferred_element_type=F32)
    act = (gc * _sigmoid(gc) * val).astype(BF16)
    part = jnp.dot(act, wout_ref[...], preferred_element_type=F32)

    @pl.when(f == 0)
    def _():
        acc_sc[...] = part

    @pl.when(f > 0)
    def _():
        acc_sc[...] += part

    @pl.when(f == pl.num_programs(1) - 1)
    def _():
        y_ref[...] = _rms(x1_ref[...] + acc_sc[...], gfin_ref[...])


def _ffn_prompt_kernel(tiles_per_seq, h2_ref, halo_ref, wg_ref, wv_ref, cw_ref, cb_ref, wout_ref, x1_ref,
                       gfin_ref, y_ref, tail_ref, acc_sc, gext_sc):
    i = pl.program_id(0)
    f = pl.program_id(1)
    tm = h2_ref.shape[0]
    h2 = h2_ref[...]
    gate = jnp.dot(h2, wg_ref[...], preferred_element_type=F32)
    gate_halo = jnp.dot(halo_ref[...], wg_ref[...], preferred_element_type=F32)
    first = (i % tiles_per_seq) == 0
    gext_sc[0:FFN_HALO, :] = jnp.where(first, 0.0, gate_halo)
    gext_sc[FFN_HALO:, :] = gate
    tail_ref[0] = gate[tm - 8:tm, :]
    gc = (cw_ref[0:1, :] * gext_sc[pl.ds(FFN_HALO - 2, tm), :]
          + cw_ref[1:2, :] * gext_sc[pl.ds(FFN_HALO - 1, tm), :]
          + cw_ref[2:3, :] * gate + cb_ref[...])
    _ffn_tail(f, gc, h2, wv_ref, wout_ref, x1_ref, gfin_ref, y_ref, acc_sc)


def _ffn_sample_kernel(dec, h2_ref, e0_ref, e1_ref, wg_ref, wv_ref, cw_ref, cb_ref, wout_ref, x1_ref,
                       gfin_ref, y_ref, gate_ref, acc_sc):
    f = pl.program_id(1)
    h2 = h2_ref[...]
    gate = jnp.dot(h2, wg_ref[...], preferred_element_type=F32)
    gate_ref[...] = gate
    pos = lax.broadcasted_iota(jnp.int32, gate.shape, 0) % dec
    g1 = jnp.where(pos == 0, e1_ref[...], pltpu.roll(gate, 1, 0))
    g2 = jnp.where(pos <= 1, e0_ref[...], pltpu.roll(gate, 2, 0))
    gc = cw_ref[0:1, :] * g2 + cw_ref[1:2, :] * g1 + cw_ref[2:3, :] * gate + cb_ref[...]
    _ffn_tail(f, gc, h2, wv_ref, wout_ref, x1_ref, gfin_ref, y_ref, acc_sc)


def _ffn_specs(tm, tf):
    nf = D_FF // tf
    return dict(
        h2=pl.BlockSpec((tm, D_MODEL), lambda i, f: (i, 0)),
        wg=pl.BlockSpec((D_MODEL, tf), lambda i, f: (0, f)),
        wv=pl.BlockSpec((D_MODEL, tf), lambda i, f: (0, nf + f)),
        cw=pl.BlockSpec((FFN_CONV_WIDTH, tf), lambda i, f: (0, f)),
        cb=pl.BlockSpec((1, tf), lambda i, f: (0, f)),
        wout=pl.BlockSpec((tf, D_MODEL), lambda i, f: (f, 0)),
        x1=pl.BlockSpec((tm, D_MODEL), lambda i, f: (i, 0)),
        gfin=pl.BlockSpec((1, D_MODEL), lambda i, f: (0, 0)),
        y=pl.BlockSpec((tm, D_MODEL), lambda i, f: (i, 0)),
    )


def _ffn_prompt(h2, x1, w_in_bf, cw, cb, w_out_bf, gfin, seq, tm):
    t = h2.shape[0]
    tf = FFN_TF
    sp = _ffn_specs(tm, tf)
    halo_per_tile = tm // FFN_HALO
    return pl.pallas_call(
        functools.partial(_ffn_prompt_kernel, seq // tm),
        grid=(t // tm, D_FF // tf),
        in_specs=[
            sp["h2"],
            pl.BlockSpec((FFN_HALO, D_MODEL), lambda i, f: (jnp.maximum(i * halo_per_tile - 1, 0), 0)),
            sp["wg"], sp["wv"], sp["cw"], sp["cb"], sp["wout"], sp["x1"], sp["gfin"],
        ],
        out_specs=[sp["y"], pl.BlockSpec((1, 8, tf), lambda i, f: (i, 0, f))],
        out_shape=[jax.ShapeDtypeStruct((t, D_MODEL), F32), jax.ShapeDtypeStruct((t // tm, 8, D_FF), F32)],
        scratch_shapes=[pltpu.VMEM((tm, D_MODEL), F32), pltpu.VMEM((tm + FFN_HALO, tf), F32)],
        compiler_params=_params(("arbitrary", "arbitrary")),
        name="ffn_prompt",
    )(h2, h2, w_in_bf, w_in_bf, cw, cb, w_out_bf, x1, gfin)


def _ffn_sample(h2, x1, e0, e1, w_in_bf, cw, cb, w_out_bf, gfin, dec):
    t = h2.shape[0]
    tf = FFN_TF
    sp = _ffn_specs(t, tf)
    hist = pl.BlockSpec((t, tf), lambda i, f: (0, f))
    return pl.pallas_call(
        functools.partial(_ffn_sample_kernel, dec),
        grid=(1, D_FF // tf),
        in_specs=[sp["h2"], hist, hist, sp["wg"], sp["wv"], sp["cw"], sp["cb"], sp["wout"], sp["x1"],
                  sp["gfin"]],
        out_specs=[sp["y"], hist],
        out_shape=[jax.ShapeDtypeStruct((t, D_MODEL), F32), jax.ShapeDtypeStruct((t, D_FF), F32)],
        scratch_shapes=[pltpu.VMEM((t, D_MODEL), F32)],
        compiler_params=_params(("arbitrary", "arbitrary")),
        name="ffn_sample",
    )(h2, e0, e1, w_in_bf, w_in_bf, cw, cb, w_out_bf, x1, gfin)


def kernel(x_prompt, x_sample, cache_k, cache_v, state_conv, state_ffn, page_table, norm_mix_g, w_in, conv_w,
           conv_b, conv_ln_g, conv_ln_b, w_conv_out, lambda_q1, lambda_k1, lambda_q2, lambda_k2, subln_g,
           w_attn_out, w_o, norm_ffn_g, w_ffn_in, ffn_conv_w, ffn_conv_b, w_ffn_out, norm_final_g):
    batch, seq, _ = x_prompt.shape
    nb, dec, _ = x_sample.shape
    n_pool = cache_k.shape[1]
    past_len = page_table.shape[1] * PAGE_SIZE
    width = N_HEADS * V_DIM
    layer = 0

    row = lambda a: a[layer].reshape(1, -1)
    w_in_bf = w_in[layer].astype(BF16)
    wco_bf = w_conv_out[layer].astype(BF16)
    wao_bf = w_attn_out[layer].astype(BF16)
    wo_bf = w_o[layer].astype(BF16)
    wfi_bf = w_ffn_in[layer].astype(BF16)
    wfo_bf = w_ffn_out[layer].astype(BF16)
    g_mix, g_ffn, g_fin = row(norm_mix_g), row(norm_ffn_g), norm_final_g.reshape(1, -1)
    cw, cb, lng, lnb = conv_w[layer], row(conv_b), row(conv_ln_g), row(conv_ln_b)
    fcw, fcb = ffn_conv_w[layer], row(ffn_conv_b)
    lams = (row(lambda_q1), row(lambda_k1), row(lambda_q2), row(lambda_k2))
    sub_g = row(subln_g)
    slopes = jnp.asarray([2.0 ** -(h + 1) for h in range(N_HEADS)], F32)

    xp = x_prompt.reshape(batch * seq, D_MODEL)
    u_p, q_p, k_p, v_p, sga_p, sgb_p = _in_proj(xp, g_mix, w_in_bf, 512, BF16)
    c_p = _conv_prompt(u_p, cw, cb, lng, lnb, seq, 512)
    od_p = _attn_prompt(q_p, k_p, v_p, slopes, lams, sub_g, batch, seq)
    x1_p, h2_p = _merge(c_p, od_p, sga_p, sgb_p, xp, wco_bf, wao_bf, wo_bf, g_ffn, 256)
    ffn_tm = 512
    y_p, tail_p = _ffn_prompt(h2_p, x1_p, wfi_bf, fcw, fcb, wfo_bf, g_fin, seq, ffn_tm)

    xs = x_sample.reshape(nb * dec, D_MODEL)
    u_s, q_s, k_s, v_s, sga_s, sgb_s = _in_proj(xs, g_mix, w_in_bf, nb * dec, F32)
    c_s, conv_s = _conv_sample(u_s.reshape(nb, dec, D_CONV), state_conv[layer], cw, cb, lng, lnb)
    cache_kt = jnp.transpose(cache_k[layer], (0, 2, 3, 4, 1)).reshape(n_pool, width, PAGE_SIZE)
    cache_vf = cache_v[layer].reshape(n_pool, PAGE_SIZE * N_HEADS, V_DIM)
    od_s = _attn_sample(q_s.reshape(nb, dec, width), k_s.reshape(nb, dec, width), v_s.reshape(nb, dec, width),
                        cache_kt, cache_vf, page_table, lams, sub_g, past_len)
    x1_s, h2_s = _merge(c_s.reshape(nb * dec, D_CONV), od_s.reshape(nb * dec, width), sga_s, sgb_s, xs,
                        wco_bf, wao_bf, wo_bf, g_ffn, nb * dec)
    st = state_ffn[layer]
    pad = lambda a: jnp.pad(a, ((0, 0), (0, dec - a.shape[1]), (0, 0))).reshape(nb * dec, D_FF)
    e0 = pad(st)
    e1 = pad(st[:, 1:2])
    y_s, gate_s = _ffn_sample(h2_s, x1_s, e0, e1, wfi_bf, fcw, fcb, wfo_bf, g_fin, dec)

    tiles_per_seq = seq // ffn_tm
    ffn_p = tail_p.reshape(batch, tiles_per_seq, 8, D_FF)[:, -1, 8 - (FFN_CONV_WIDTH - 1):, :]
    return (
        y_p.reshape(batch, seq, D_MODEL),
        y_s.reshape(nb, dec, D_MODEL),
        k_p.reshape(1, batch, seq, N_HEADS, 2, HEAD_DIM),
        v_p.reshape(1, batch, seq, N_HEADS, V_DIM),
        u_p.reshape(batch, seq, D_CONV)[None, :, seq - (CONV_WIDTH - 1):, :],
        ffn_p[None],
        k_s.reshape(1, nb, dec, N_HEADS, 2, HEAD_DIM),
        v_s.reshape(1, nb, dec, N_HEADS, V_DIM),
        conv_s[None],
        gate_s.reshape(nb, dec, D_FF)[None, :, dec - (FFN_CONV_WIDTH - 1):, :],
    )
```

```python
import functools
import math

import jax
import jax.numpy as jnp
from jax import lax
from jax.experimental import pallas as pl
from jax.experimental.pallas import tpu as pltpu

D_MODEL = 2048
N_HEADS = 8
HEAD_DIM = 64
V_DIM = 2 * HEAD_DIM
D_CONV = D_MODEL // 2
CONV_WIDTH = 31
D_FF = 5632
FFN_CONV_WIDTH = 3
EPS = 1e-5
NEG_INF = -1e30
PAGE_SIZE = 128
LAM_INIT = 0.8 - 0.6 * math.exp(-0.3 * 0)

F32 = jnp.float32
BF16 = jnp.bfloat16

VMEM_LIMIT_BYTES = 56 * 1024 * 1024

PROJ_TN = 1024
ATT_TQ = 256
PAGES_PER_STEP = 8
FFN_TF = 512
FFN_HALO = 16
CONV_HALO = 32


def _params(sem):
    return pltpu.CompilerParams(dimension_semantics=sem, vmem_limit_bytes=VMEM_LIMIT_BYTES)


def _sigmoid(x):
    return 1.0 / (1.0 + jnp.exp(-x))


def _rms(x, g):
    return x * lax.rsqrt(jnp.mean(x * x, axis=-1, keepdims=True) + EPS) * g


def _rms_cast_kernel(x_ref, g_ref, h_ref):
    h_ref[...] = _rms(x_ref[...], g_ref[...]).astype(BF16)


def _rms_cast(x, g, tm):
    t = x.shape[0]
    return pl.pallas_call(
        _rms_cast_kernel,
        grid=(t // tm,),
        in_specs=[pl.BlockSpec((tm, D_MODEL), lambda i: (i, 0)), pl.BlockSpec((1, D_MODEL), lambda i: (0, 0))],
        out_specs=pl.BlockSpec((tm, D_MODEL), lambda i: (i, 0)),
        out_shape=jax.ShapeDtypeStruct((t, D_MODEL), BF16),
        compiler_params=_params(("arbitrary",)),
        name="rms_cast",
    )(x, g)


def _glu_kernel(h_ref, wv_ref, wg_ref, u_ref):
    h = h_ref[...]
    u_ref[...] = jnp.dot(h, wv_ref[...], preferred_element_type=F32) * _sigmoid(
        jnp.dot(h, wg_ref[...], preferred_element_type=F32))


def _qkv_kernel(h_ref, w_ref, o_ref):
    scale = jnp.where(pl.program_id(1) == 0, HEAD_DIM ** -0.5, 1.0)
    o_ref[...] = jnp.dot(h_ref[...], w_ref[...], preferred_element_type=F32) * scale


def _gates_kernel(h_ref, w_ref, o_ref):
    o_ref[...] = _sigmoid(jnp.dot(h_ref[...], w_ref[...], preferred_element_type=F32))


def _proj(kern, name, h, w_bf, tm, tn, n_col, col_tile0):
    t = h.shape[0]
    w_specs = [pl.BlockSpec((D_MODEL, tn), functools.partial(lambda c0, i, j: (0, c0 + j), c0))
               for c0 in col_tile0]
    return pl.pallas_call(
        kern,
        grid=(t // tm, n_col),
        in_specs=[pl.BlockSpec((tm, D_MODEL), lambda i, j: (i, 0))] + w_specs,
        out_specs=pl.BlockSpec((tm, tn), lambda i, j: (i, j)),
        out_shape=jax.ShapeDtypeStruct((t, tn * n_col), F32),
        compiler_params=_params(("arbitrary", "arbitrary")),
        name=name,
    )(h, *([w_bf] * len(col_tile0)))


def _in_proj(x, g, w_bf, tm):
    h = _rms_cast(x, g, tm)
    glu_tn = 512
    u = _proj(_glu_kernel, "proj_glu", h, w_bf, tm, glu_tn, D_CONV // glu_tn, [0, D_CONV // glu_tn])
    tn = PROJ_TN
    qkv0 = 2 * D_CONV // tn
    qkv = _proj(_qkv_kernel, "proj_qkv", h, w_bf, tm, tn, 3, [qkv0])
    gates = _proj(_gates_kernel, "proj_gates", h, w_bf, tm, tn, 2 * D_MODEL // tn, [qkv0 + 3])
    return u, qkv, gates


def _ln_swish(c, g, b):
    mu = jnp.mean(c, axis=-1, keepdims=True)
    cc = c - mu
    y = cc * lax.rsqrt(jnp.mean(cc * cc, axis=-1, keepdims=True) + EPS) * g + b
    return y * _sigmoid(y)


def _conv_prompt_kernel(tiles_per_seq, rows_chunk, u_ref, halo_ref, w_ref, b_ref, g_ref, beta_ref,
                        o_ref, ext_sc, c_sc):
    i = pl.program_id(0)
    tm = u_ref.shape[0]
    first = (i % tiles_per_seq) == 0
    ext_sc[0:CONV_HALO, :] = jnp.where(first, 0.0, halo_ref[...])
    ext_sc[CONV_HALO:, :] = u_ref[...]
    lead = CONV_HALO - (CONV_WIDTH - 1)
    col_chunk = 256

    def body(r, carry):
        r0 = pl.multiple_of(r * rows_chunk, rows_chunk)
        for c0 in range(0, D_CONV, col_chunk):
            acc = jnp.broadcast_to(b_ref[:, c0:c0 + col_chunk], (rows_chunk, col_chunk))
            win = ext_sc[pl.ds(r0, rows_chunk + CONV_HALO), c0:c0 + col_chunk]
            for tap in range(CONV_WIDTH):
                acc = acc + w_ref[tap:tap + 1, c0:c0 + col_chunk] * win[lead + tap:lead + tap + rows_chunk]
            c_sc[pl.ds(r0, rows_chunk), c0:c0 + col_chunk] = acc
        return carry

    lax.fori_loop(0, tm // rows_chunk, body, 0)
    o_ref[...] = _ln_swish(c_sc[...], g_ref[...], beta_ref[...]).astype(o_ref.dtype)


def _conv_prompt(u, w, b, g, beta, seq, tm):
    t = u.shape[0]
    tiles_per_seq = seq // tm
    halo_per_tile = tm // CONV_HALO
    vec = lambda i: (0, 0)
    return pl.pallas_call(
        functools.partial(_conv_prompt_kernel, tiles_per_seq, 32),
        grid=(t // tm,),
        in_specs=[
            pl.BlockSpec((tm, D_CONV), lambda i: (i, 0)),
            pl.BlockSpec((CONV_HALO, D_CONV), lambda i: (jnp.maximum(i * halo_per_tile - 1, 0), 0)),
            pl.BlockSpec((CONV_WIDTH, D_CONV), vec),
            pl.BlockSpec((1, D_CONV), vec),
            pl.BlockSpec((1, D_CONV), vec),
            pl.BlockSpec((1, D_CONV), vec),
        ],
        out_specs=pl.BlockSpec((tm, D_CONV), lambda i: (i, 0)),
        out_shape=jax.ShapeDtypeStruct((t, D_CONV), BF16),
        scratch_shapes=[pltpu.VMEM((tm + CONV_HALO, D_CONV), F32), pltpu.VMEM((tm, D_CONV), F32)],
        compiler_params=_params(("arbitrary",)),
        name="conv_prompt",
    )(u, u, w, b, g, beta)


def _conv_sample_kernel(u_ref, st_ref, w_ref, b_ref, g_ref, beta_ref, o_ref, new_ref, ext_sc):
    nb, dec, _ = u_ref.shape
    hist = CONV_WIDTH - 1
    ext_sc[:, 0:hist, :] = st_ref[...]
    ext_sc[:, hist:hist + dec, :] = u_ref[...]
    acc = jnp.broadcast_to(b_ref[...][None], (nb, dec, D_CONV))
    for tap in range(CONV_WIDTH):
        acc = acc + w_ref[tap:tap + 1, :][None] * ext_sc[:, tap:tap + dec, :]
    o_ref[...] = _ln_swish(acc, g_ref[...][None], beta_ref[...][None]).astype(o_ref.dtype)
    new_ref[...] = ext_sc[:, dec:dec + hist, :]


def _conv_sample(u3, state, w, b, g, beta):
    nb, dec, _ = u3.shape
    hist = CONV_WIDTH - 1
    return pl.pallas_call(
        _conv_sample_kernel,
        out_shape=[
            jax.ShapeDtypeStruct((nb, dec, D_CONV), BF16),
            jax.ShapeDtypeStruct((nb, hist, D_CONV), F32),
        ],
        scratch_shapes=[pltpu.VMEM((nb, hist + dec, D_CONV), F32)],
        compiler_params=pltpu.CompilerParams(vmem_limit_bytes=VMEM_LIMIT_BYTES),
        name="conv_sample",
    )(u3, state, w, b, g, beta)


NT_DIMS = (((1,), (1,)), ((), ()))


def _lambda(lq1, lk1, lq2, lk2):
    return (jnp.exp(jnp.sum(lq1 * lk1, axis=-1, keepdims=True))
            - jnp.exp(jnp.sum(lq2 * lk2, axis=-1, keepdims=True)) + LAM_INIT)


def _stack_maps(q):
    first = lax.broadcasted_iota(jnp.int32, q.shape, 1) < HEAD_DIM
    zero = jnp.zeros_like(q)
    return jnp.concatenate([jnp.where(first, q, zero), jnp.where(first, zero, q)], axis=0)


def _attn_prompt_kernel(slopes_ref, q_ref, k_ref, v_ref, lq1, lk1, lq2, lk2, gcol_ref, o_ref,
                        kb_sc, vt_sc, q2_sc, bias_sc, bdiag_sc, s_sc, p_sc, alpha_sc, m_sc, l_sc, acc_sc):
    h = pl.program_id(1)
    qi = pl.program_id(2)
    tq = q_ref.shape[0]
    slope = slopes_ref[h]

    @pl.when(qi == 0)
    def _():
        kb_sc[...] = k_ref[...].astype(BF16)
        for kb in range(k_ref.shape[0] // tq):
            vt_sc[kb] = v_ref[kb * tq:(kb + 1) * tq, :].T.astype(BF16)
        key = lax.broadcasted_iota(jnp.int32, (tq, 2 * tq), 0)
        qry = lax.broadcasted_iota(jnp.int32, (tq, 2 * tq), 1)
        dist = (jnp.where(qry >= tq, qry - tq, qry) - key).astype(F32)
        bias_sc[...] = slope * dist
        bdiag_sc[...] = jnp.where(dist >= 0.0, slope * dist, -NEG_INF)

    q2_sc[...] = _stack_maps(q_ref[...]).astype(BF16)
    m_sc[...] = jnp.full_like(m_sc, -jnp.inf)
    l_sc[...] = jnp.zeros_like(l_sc)
    acc_sc[...] = jnp.zeros_like(acc_sc)

    def scores(kb):
        start = pl.multiple_of(kb * tq, tq)
        return lax.dot_general(kb_sc[pl.ds(start, tq), :], q2_sc[...], NT_DIMS, preferred_element_type=F32)

    def weighted_values(kb):
        acc_sc[...] = alpha_sc[...] * acc_sc[...] + jnp.dot(vt_sc[kb], p_sc[...], preferred_element_type=F32)

    def softmax_step(s_raw, bias_ref, beta):
        s = s_raw - bias_ref[...]
        m_prev = m_sc[...]
        m_new = jnp.maximum(m_prev, jnp.max(s, axis=0, keepdims=True) - beta)
        alpha = jnp.exp(m_prev - m_new)
        p = jnp.exp(s - (m_new + beta))
        l_sc[...] = alpha * l_sc[...] + jnp.sum(p, axis=0, keepdims=True)
        m_sc[...] = m_new
        alpha_sc[...] = alpha
        p_sc[...] = p.astype(BF16)

    s_sc[...] = scores(0)
    p_sc[...] = jnp.zeros_like(p_sc)
    alpha_sc[...] = jnp.ones_like(alpha_sc)

    def body(kb, carry):
        s_raw = s_sc[...]
        s_sc[...] = scores(kb + 1)
        weighted_values(jnp.maximum(kb - 1, 0))
        softmax_step(s_raw, bias_sc, slope * ((qi - kb) * tq).astype(F32))
        return carry

    lax.fori_loop(0, qi, body, 0)
    weighted_values(jnp.maximum(qi - 1, 0))
    softmax_step(s_sc[...], bdiag_sc, 0.0)
    weighted_values(qi)

    lam = _lambda(lq1[...], lk1[...], lq2[...], lk2[...])
    o = acc_sc[...] / l_sc[...]
    od = o[:, 0:tq] - lam * o[:, tq:2 * tq]
    od = od * lax.rsqrt(jnp.mean(od * od, axis=0, keepdims=True) + EPS) * gcol_ref[...] * (1.0 - LAM_INIT)
    o_ref[...] = od.T.astype(o_ref.dtype)


def _attn_prompt(q, k, v, col0, slopes, lams, subln_g, batch, seq):
    tq = ATT_TQ
    nq = seq // tq
    vec = lambda b, h, i: (0, 0)
    qc, kc, vc = col0
    return pl.pallas_call(
        _attn_prompt_kernel,
        grid=(batch, N_HEADS, nq),
        in_specs=[
            pl.BlockSpec(memory_space=pltpu.SMEM),
            pl.BlockSpec((tq, V_DIM), lambda b, h, i: (b * nq + i, qc + h)),
            pl.BlockSpec((seq, V_DIM), lambda b, h, i: (b, kc + h)),
            pl.BlockSpec((seq, V_DIM), lambda b, h, i: (b, vc + h)),
            pl.BlockSpec((1, HEAD_DIM), vec),
            pl.BlockSpec((1, HEAD_DIM), vec),
            pl.BlockSpec((1, HEAD_DIM), vec),
            pl.BlockSpec((1, HEAD_DIM), vec),
            pl.BlockSpec((V_DIM, 1), vec),
        ],
        out_specs=pl.BlockSpec((tq, V_DIM), lambda b, h, i: (b * nq + i, h)),
        out_shape=jax.ShapeDtypeStruct((batch * seq, N_HEADS * V_DIM), BF16),
        scratch_shapes=[
            pltpu.VMEM((seq, V_DIM), BF16),
            pltpu.VMEM((nq, V_DIM, tq), BF16),
            pltpu.VMEM((2 * tq, V_DIM), BF16),
            pltpu.VMEM((tq, 2 * tq), F32),
            pltpu.VMEM((tq, 2 * tq), F32),
            pltpu.VMEM((tq, 2 * tq), F32),
            pltpu.VMEM((tq, 2 * tq), BF16),
            pltpu.VMEM((1, 2 * tq), F32),
            pltpu.VMEM((1, 2 * tq), F32),
            pltpu.VMEM((1, 2 * tq), F32),
            pltpu.VMEM((V_DIM, 2 * tq), F32),
        ],
        compiler_params=_params(("arbitrary", "arbitrary", "arbitrary")),
        name="attn_prompt",
    )(slopes, q, k, v, *lams, subln_g.reshape(V_DIM, 1))


def _attn_sample_kernel(past_len, pt_ref, q_ref, kn_ref, vn_ref, *rest):
    npg = PAGES_PER_STEP
    k_refs = rest[0:npg]
    v_refs = rest[npg:2 * npg]
    lq1, lk1, lq2, lk2, g_ref, o_ref, kt_sc, vb_sc, qbd_sc, sb_sc, m_sc, l_sc, acc_sc = rest[2 * npg:]
    j = pl.program_id(1)
    dec = q_ref.shape[1]
    width = q_ref.shape[2]
    rows = 2 * N_HEADS * dec
    rows_per_head = 2 * dec
    keys = npg * PAGE_SIZE
    chunk = 256
    rows_per_chunk = (chunk // HEAD_DIM) * dec

    row1 = lax.broadcasted_iota(jnp.int32, (rows, 1), 0)
    slope = jnp.zeros((rows, 1), F32)
    for h in range(N_HEADS):
        slope = jnp.where(row1 // rows_per_head == h, 2.0 ** -(h + 1), slope)

    @pl.when(j == 0)
    def _():
        m_sc[...] = jnp.full_like(m_sc, -jnp.inf)
        l_sc[...] = jnp.zeros_like(l_sc)
        acc_sc[...] = jnp.zeros_like(acc_sc)
        qt = jnp.concatenate([q_ref[0]] * (rows // dec), axis=0)
        rr = lax.broadcasted_iota(jnp.int32, (rows, width), 0)
        cc = lax.broadcasted_iota(jnp.int32, (rows, width), 1)
        qbd_sc[...] = jnp.where(cc // HEAD_DIM == rr // dec, qt, 0.0).astype(BF16)
        qpos = lax.broadcasted_iota(jnp.int32, (rows, keys), 0) % dec
        col = lax.broadcasted_iota(jnp.int32, (rows, keys), 1)
        sb_sc[...] = slope * (past_len + qpos - col).astype(F32)

    for r in range(npg):
        kt_sc[:, r * PAGE_SIZE:(r + 1) * PAGE_SIZE] = k_refs[r][0].astype(BF16)
        for h in range(N_HEADS):
            vb_sc[h, r * PAGE_SIZE:(r + 1) * PAGE_SIZE, :] = v_refs[r][
                0, pl.ds(h, PAGE_SIZE, stride=N_HEADS), :].astype(BF16)

    gamma = slope * (j * keys).astype(F32)

    def update(r0, nrows, s, g, pv_fn):
        sl = slice(r0, r0 + nrows)
        m_prev = m_sc[sl]
        m_new = jnp.maximum(m_prev, jnp.max(s, axis=-1, keepdims=True) + g)
        alpha = jnp.exp(m_prev - m_new)
        p = jnp.exp(s - (m_new - g))
        l_sc[sl] = alpha * l_sc[sl] + jnp.sum(p, axis=-1, keepdims=True)
        acc_sc[sl] = alpha * acc_sc[sl] + pv_fn(p.astype(BF16))
        m_sc[sl] = m_new

    n_chunks = width // chunk
    heads_per_chunk = rows_per_chunk // rows_per_head
    scores = [jnp.dot(qbd_sc[c * rows_per_chunk:(c + 1) * rows_per_chunk, c * chunk:(c + 1) * chunk],
                      kt_sc[c * chunk:(c + 1) * chunk, :], preferred_element_type=F32) for c in range(n_chunks)]
    probs, alphas = [], []
    for c in range(n_chunks):
        sl = slice(c * rows_per_chunk, (c + 1) * rows_per_chunk)
        s = scores[c] - sb_sc[sl, :]
        g = gamma[sl]
        m_prev = m_sc[sl]
        m_new = jnp.maximum(m_prev, jnp.max(s, axis=-1, keepdims=True) + g)
        alpha = jnp.exp(m_prev - m_new)
        p = jnp.exp(s - (m_new - g))
        l_sc[sl] = alpha * l_sc[sl] + jnp.sum(p, axis=-1, keepdims=True)
        m_sc[sl] = m_new
        probs.append(p.astype(BF16))
        alphas.append(alpha)
    for c in range(n_chunks):
        sl = slice(c * rows_per_chunk, (c + 1) * rows_per_chunk)
        pv = jnp.concatenate(
            [jnp.dot(probs[c][i * rows_per_head:(i + 1) * rows_per_head], vb_sc[c * heads_per_chunk + i],
                     preferred_element_type=F32) for i in range(heads_per_chunk)], axis=0)
        acc_sc[sl] = alphas[c] * acc_sc[sl] + pv

    @pl.when(j == pl.num_programs(1) - 1)
    def _():
        fill = jnp.zeros((PAGE_SIZE - dec, width), F32)
        k_new = jnp.concatenate([kn_ref[0], fill], axis=0).astype(BF16)
        v_new = jnp.concatenate([vn_ref[0], fill], axis=0).astype(BF16)
        qpos = lax.broadcasted_iota(jnp.int32, (rows, PAGE_SIZE), 0) % dec
        col = lax.broadcasted_iota(jnp.int32, (rows, PAGE_SIZE), 1)
        dist = (qpos - col).astype(F32)
        s = lax.dot_general(qbd_sc[...], k_new, NT_DIMS, preferred_element_type=F32)
        s = jnp.where(dist >= 0.0, s - slope * dist, NEG_INF)

        def pv(pb):
            return jnp.concatenate(
                [jnp.dot(pb[h * rows_per_head:(h + 1) * rows_per_head], v_new[:, h * V_DIM:(h + 1) * V_DIM],
                         preferred_element_type=F32) for h in range(N_HEADS)], axis=0)

        update(0, rows, s, 0.0, pv)

        lam = _lambda(lq1[...], lk1[...], lq2[...], lk2[...])
        o = acc_sc[...] / l_sc[...]
        for h in range(N_HEADS):
            r0 = h * rows_per_head
            od = o[r0:r0 + dec] - lam * o[r0 + dec:r0 + 2 * dec]
            o_ref[0, :, h * V_DIM:(h + 1) * V_DIM] = (_rms(od, g_ref[...]) * (1.0 - LAM_INIT)).astype(o_ref.dtype)


def _attn_sample(q3, k3, v3, col0, cache_kt, cache_vf, page_table, lams, subln_g, past_len):
    nb, dec, _ = q3.shape
    width = N_HEADS * V_DIM
    npg = PAGES_PER_STEP
    n_pages = page_table.shape[1]
    rows = 2 * N_HEADS * dec
    keys = npg * PAGE_SIZE
    tok = pl.BlockSpec((1, dec, width), lambda b, j, pt: (b, 0, 0))
    toks = [pl.BlockSpec((1, dec, width), functools.partial(lambda c, b, j, pt: (b, 0, c), c)) for c in col0]
    vec = lambda b, j, pt: (0, 0)

    def page_spec(r, shape):
        return pl.BlockSpec((1,) + shape, lambda b, j, pt: (pt[b * n_pages + j * npg + r], 0, 0))

    grid_spec = pltpu.PrefetchScalarGridSpec(
        num_scalar_prefetch=1,
        grid=(nb, n_pages // npg),
        in_specs=toks
        + [page_spec(r, cache_kt.shape[1:]) for r in range(npg)]
        + [page_spec(r, cache_vf.shape[1:]) for r in range(npg)]
        + [pl.BlockSpec((1, HEAD_DIM), vec)] * 4
        + [pl.BlockSpec((1, V_DIM), vec)],
        out_specs=tok,
        scratch_shapes=[
            pltpu.VMEM((width, keys), BF16),
            pltpu.VMEM((N_HEADS, keys, V_DIM), BF16),
            pltpu.VMEM((rows, width), BF16),
            pltpu.VMEM((rows, keys), F32),
            pltpu.VMEM((rows, 1), F32),
            pltpu.VMEM((rows, 1), F32),
            pltpu.VMEM((rows, V_DIM), F32),
        ],
    )
    return pl.pallas_call(
        functools.partial(_attn_sample_kernel, past_len),
        grid_spec=grid_spec,
        out_shape=jax.ShapeDtypeStruct((nb, dec, width), F32),
        compiler_params=_params(("arbitrary", "arbitrary")),
        name="attn_sample",
    )(page_table.reshape(-1), q3, k3, v3, *([cache_kt] * npg), *([cache_vf] * npg), *lams, subln_g)


def _merge_kernel(c_ref, od_ref, ga_ref, gb_ref, x_ref, wco_ref, wao_ref, wo_ref, g_ref, x1_ref, h2_ref):
    out_a = jnp.dot(c_ref[...].astype(BF16), wco_ref[...], preferred_element_type=F32)
    out_b = jnp.dot(od_ref[...].astype(BF16), wao_ref[...], preferred_element_type=F32)
    merged = ga_ref[...] * out_a + gb_ref[...] * out_b
    x1 = x_ref[...] + jnp.dot(merged.astype(BF16), wo_ref[...], preferred_element_type=F32)
    x1_ref[...] = x1
    h2_ref[...] = _rms(x1, g_ref[...]).astype(BF16)


def _merge(c_act, od, gates, x, wco, wao, wo, g, tm):
    t = x.shape[0]
    row = lambda i: (i, 0)
    fixed = lambda i: (0, 0)
    resident = functools.partial(pl.BlockSpec, index_map=fixed, pipeline_mode=pl.Buffered(1))
    return pl.pallas_call(
        _merge_kernel,
        grid=(t // tm,),
        in_specs=[
            pl.BlockSpec((tm, D_CONV), row),
            pl.BlockSpec((tm, N_HEADS * V_DIM), row),
            pl.BlockSpec((tm, D_MODEL), row),
            pl.BlockSpec((tm, D_MODEL), lambda i: (i, 1)),
            pl.BlockSpec((tm, D_MODEL), row),
            resident((D_CONV, D_MODEL)),
            resident((N_HEADS * V_DIM, D_MODEL)),
            resident((D_MODEL, D_MODEL)),
            pl.BlockSpec((1, D_MODEL), fixed),
        ],
        out_specs=[pl.BlockSpec((tm, D_MODEL), row), pl.BlockSpec((tm, D_MODEL), row)],
        out_shape=[jax.ShapeDtypeStruct((t, D_MODEL), F32), jax.ShapeDtypeStruct((t, D_MODEL), BF16)],
        compiler_params=_params(("arbitrary",)),
        name="merge",
    )(c_act, od, gates, gates, x, wco, wao, wo, g)


def _ffn_act(gc, h2, wv_ref):
    val = jnp.dot(h2, wv_ref[...], preferred_element_type=F32)
    return (gc * _sigmoid(gc) * val).astype(BF16)


def _ffn_up_prompt_kernel(tiles_per_seq, h2_ref, halo_ref, wg_ref, wv_ref, cw_ref, cb_ref, act_ref, tail_ref,
                          gext_sc):
    i = pl.program_id(0)
    tm = h2_ref.shape[0]
    h2 = h2_ref[...]
    gate = jnp.dot(h2, wg_ref[...], preferred_element_type=F32)
    gate_halo = jnp.dot(halo_ref[...], wg_ref[...], preferred_element_type=F32)
    first = (i % tiles_per_seq) == 0
    gext_sc[0:FFN_HALO, :] = jnp.where(first, 0.0, gate_halo)
    gext_sc[FFN_HALO:, :] = gate
    tail_ref[0] = gate[tm - 8:tm, :]
    gc = (cw_ref[0:1, :] * gext_sc[pl.ds(FFN_HALO - 2, tm), :]
          + cw_ref[1:2, :] * gext_sc[pl.ds(FFN_HALO - 1, tm), :]
          + cw_ref[2:3, :] * gate + cb_ref[...])
    act_ref[...] = _ffn_act(gc, h2, wv_ref)


def _ffn_up_sample_kernel(dec, h2_ref, e0_ref, e1_ref, wg_ref, wv_ref, cw_ref, cb_ref, act_ref, gate_ref):
    h2 = h2_ref[...]
    gate = jnp.dot(h2, wg_ref[...], preferred_element_type=F32)
    gate_ref[...] = gate
    pos = lax.broadcasted_iota(jnp.int32, gate.shape, 0) % dec
    g1 = jnp.where(pos == 0, e1_ref[...], pltpu.roll(gate, 1, 0))
    g2 = jnp.where(pos <= 1, e0_ref[...], pltpu.roll(gate, 2, 0))
    gc = cw_ref[0:1, :] * g2 + cw_ref[1:2, :] * g1 + cw_ref[2:3, :] * gate + cb_ref[...]
    act_ref[...] = _ffn_act(gc, h2, wv_ref)


def _ffn_up_specs(tm, tf):
    nf = D_FF // tf
    return dict(
        h2=pl.BlockSpec((tm, D_MODEL), lambda i, f: (i, 0)),
        wg=pl.BlockSpec((D_MODEL, tf), lambda i, f: (0, f)),
        wv=pl.BlockSpec((D_MODEL, tf), lambda i, f: (0, nf + f)),
        cw=pl.BlockSpec((FFN_CONV_WIDTH, tf), lambda i, f: (0, f)),
        cb=pl.BlockSpec((1, tf), lambda i, f: (0, f)),
        act=pl.BlockSpec((tm, tf), lambda i, f: (i, f)),
    )


def _ffn_up_prompt(h2, w_in_bf, cw, cb, seq, tm):
    t = h2.shape[0]
    tf = FFN_TF
    sp = _ffn_up_specs(tm, tf)
    halo_per_tile = tm // FFN_HALO
    return pl.pallas_call(
        functools.partial(_ffn_up_prompt_kernel, seq // tm),
        grid=(t // tm, D_FF // tf),
        in_specs=[
            sp["h2"],
            pl.BlockSpec((FFN_HALO, D_MODEL), lambda i, f: (jnp.maximum(i * halo_per_tile - 1, 0), 0)),
            sp["wg"], sp["wv"], sp["cw"], sp["cb"],
        ],
        out_specs=[sp["act"], pl.BlockSpec((1, 8, tf), lambda i, f: (i, 0, f))],
        out_shape=[jax.ShapeDtypeStruct((t, D_FF), BF16), jax.ShapeDtypeStruct((t // tm, 8, D_FF), F32)],
        scratch_shapes=[pltpu.VMEM((tm + FFN_HALO, tf), F32)],
        compiler_params=_params(("arbitrary", "arbitrary")),
        name="ffn_up_prompt",
    )(h2, h2, w_in_bf, w_in_bf, cw, cb)


def _ffn_up_sample(h2, e0, e1, w_in_bf, cw, cb, dec):
    t = h2.shape[0]
    tf = FFN_TF
    sp = _ffn_up_specs(t, tf)
    hist = pl.BlockSpec((t, tf), lambda i, f: (0, f))
    return pl.pallas_call(
        functools.partial(_ffn_up_sample_kernel, dec),
        grid=(1, D_FF // tf),
        in_specs=[sp["h2"], hist, hist, sp["wg"], sp["wv"], sp["cw"], sp["cb"]],
        out_specs=[sp["act"], hist],
        out_shape=[jax.ShapeDtypeStruct((t, D_FF), BF16), jax.ShapeDtypeStruct((t, D_FF), F32)],
        compiler_params=_params(("arbitrary", "arbitrary")),
        name="ffn_up_sample",
    )(h2, e0, e1, w_in_bf, w_in_bf, cw, cb)


def _ffn_down_kernel(act_ref, w_ref, x1_ref, g_ref, y_ref):
    y_ref[...] = _rms(x1_ref[...] + jnp.dot(act_ref[...], w_ref[...], preferred_element_type=F32), g_ref[...])


def _ffn_down(act, x1, w_out_bf, gfin, tm):
    t = act.shape[0]
    row = lambda i: (i, 0)
    fixed = lambda i: (0, 0)
    return pl.pallas_call(
        _ffn_down_kernel,
        grid=(t // tm,),
        in_specs=[
            pl.BlockSpec((tm, D_FF), row),
            pl.BlockSpec((D_FF, D_MODEL), fixed, pipeline_mode=pl.Buffered(1)),
            pl.BlockSpec((tm, D_MODEL), row),
            pl.BlockSpec((1, D_MODEL), fixed),
        ],
        out_specs=pl.BlockSpec((tm, D_MODEL), row),
        out_shape=jax.ShapeDtypeStruct((t, D_MODEL), F32),
        compiler_params=_params(("arbitrary",)),
        name="ffn_down",
    )(act, w_out_bf, x1, gfin)


def kernel(x_prompt, x_sample, cache_k, cache_v, state_conv, state_ffn, page_table, norm_mix_g, w_in, conv_w,
           conv_b, conv_ln_g, conv_ln_b, w_conv_out, lambda_q1, lambda_k1, lambda_q2, lambda_k2, subln_g,
           w_attn_out, w_o, norm_ffn_g, w_ffn_in, ffn_conv_w, ffn_conv_b, w_ffn_out, norm_final_g):
    batch, seq, _ = x_prompt.shape
    nb, dec, _ = x_sample.shape
    n_pool = cache_k.shape[1]
    past_len = page_table.shape[1] * PAGE_SIZE
    width = N_HEADS * V_DIM
    layer = 0

    row = lambda a: a[layer].reshape(1, -1)
    w_in_bf = w_in[layer].astype(BF16)
    wco_bf = w_conv_out[layer].astype(BF16)
    wao_bf = w_attn_out[layer].astype(BF16)
    wo_bf = w_o[layer].astype(BF16)
    wfi_bf = w_ffn_in[layer].astype(BF16)
    wfo_bf = w_ffn_out[layer].astype(BF16)
    g_mix, g_ffn, g_fin = row(norm_mix_g), row(norm_ffn_g), norm_final_g.reshape(1, -1)
    cw, cb, lng, lnb = conv_w[layer], row(conv_b), row(conv_ln_g), row(conv_ln_b)
    fcw, fcb = ffn_conv_w[layer], row(ffn_conv_b)
    lams = (row(lambda_q1), row(lambda_k1), row(lambda_q2), row(lambda_k2))
    sub_g = row(subln_g)
    slopes = jnp.asarray([2.0 ** -(h + 1) for h in range(N_HEADS)], F32)

    xp = x_prompt.reshape(batch * seq, D_MODEL)
    u_p, qkv_p, gates_p = _in_proj(xp, g_mix, w_in_bf, 512)
    c_p = _conv_prompt(u_p, cw, cb, lng, lnb, seq, 512)
    od_p = _attn_prompt(qkv_p, qkv_p, qkv_p, (0, N_HEADS, 2 * N_HEADS), slopes, lams, sub_g, batch, seq)
    x1_p, h2_p = _merge(c_p, od_p, gates_p, xp, wco_bf, wao_bf, wo_bf, g_ffn, 256)
    ffn_tm = 1024
    act_p, tail_p = _ffn_up_prompt(h2_p, wfi_bf, fcw, fcb, seq, ffn_tm)
    y_p = _ffn_down(act_p, x1_p, wfo_bf, g_fin, 256)

    xs = x_sample.reshape(nb * dec, D_MODEL)
    u_s, qkv_s, gates_s = _in_proj(xs, g_mix, w_in_bf, nb * dec)
    c_s, conv_s = _conv_sample(u_s.reshape(nb, dec, D_CONV), state_conv[layer], cw, cb, lng, lnb)
    cache_kt = jnp.transpose(cache_k[layer], (0, 2, 3, 4, 1)).reshape(n_pool, width, PAGE_SIZE)
    cache_vf = cache_v[layer].reshape(n_pool, PAGE_SIZE * N_HEADS, V_DIM)
    qkv_s3 = qkv_s.reshape(nb, dec, 3 * width)
    od_s = _attn_sample(qkv_s3, qkv_s3, qkv_s3, (0, 1, 2), cache_kt, cache_vf, page_table, lams, sub_g, past_len)
    x1_s, h2_s = _merge(c_s.reshape(nb * dec, D_CONV), od_s.reshape(nb * dec, width), gates_s, xs,
                        wco_bf, wao_bf, wo_bf, g_ffn, nb * dec)
    st = state_ffn[layer]
    pad = lambda a: jnp.pad(a, ((0, 0), (0, dec - a.shape[1]), (0, 0))).reshape(nb * dec, D_FF)
    e0 = pad(st)
    e1 = pad(st[:, 1:2])
    act_s, gate_s = _ffn_up_sample(h2_s, e0, e1, wfi_bf, fcw, fcb, dec)
    y_s = _ffn_down(act_s, x1_s, wfo_bf, g_fin, nb * dec)

    tiles_per_seq = seq // ffn_tm
    ffn_p = tail_p.reshape(batch, tiles_per_seq, 8, D_FF)[:, -1, 8 - (FFN_CONV_WIDTH - 1):, :]
    return (
        y_p.reshape(batch, seq, D_MODEL),
        y_s.reshape(nb, dec, D_MODEL),
        qkv_p[:, width:2 * width].reshape(1, batch, seq, N_HEADS, 2, HEAD_DIM),
        qkv_p[:, 2 * width:].reshape(1, batch, seq, N_HEADS, V_DIM),
        u_p.reshape(batch, seq, D_CONV)[None, :, seq - (CONV_WIDTH - 1):, :],
        ffn_p[None],
        qkv_s[:, width:2 * width].reshape(1, nb, dec, N_HEADS, 2, HEAD_DIM),
        qkv_s[:, 2 * width:].reshape(1, nb, dec, N_HEADS, V_DIM),
        conv_s[None],
        gate_s.reshape(nb, dec, D_FF)[None, :, dec - (FFN_CONV_WIDTH - 1):, :],
    )
```

```python
import functools
import math

import jax
import jax.numpy as jnp
from jax import lax
from jax.experimental import pallas as pl
from jax.experimental.pallas import tpu as pltpu

D_MODEL = 2048
N_HEADS = 8
HEAD_DIM = 64
V_DIM = 2 * HEAD_DIM
D_CONV = D_MODEL // 2
CONV_WIDTH = 31
D_FF = 5632
FFN_CONV_WIDTH = 3
EPS = 1e-5
NEG_INF = -1e30
PAGE_SIZE = 128
LAM_INIT = 0.8 - 0.6 * math.exp(-0.3 * 0)

F32 = jnp.float32
BF16 = jnp.bfloat16

VMEM_LIMIT_BYTES = 56 * 1024 * 1024

PROJ_TN = 1024
ATT_TQ = 256
PAGES_PER_STEP = 16
FFN_TF = 512
FFN_HALO = 16
CONV_HALO = 32


NT_DIMS = (((1,), (1,)), ((), ()))


def _params(sem):
    return pltpu.CompilerParams(dimension_semantics=sem, vmem_limit_bytes=VMEM_LIMIT_BYTES)


def _sigmoid(x):
    return 1.0 / (1.0 + jnp.exp(-x))


def _rms(x, g):
    return x * lax.rsqrt(jnp.mean(x * x, axis=-1, keepdims=True) + EPS) * g


def _rms_cast_kernel(x_ref, g_ref, h_ref):
    h_ref[...] = _rms(x_ref[...], g_ref[...]).astype(BF16)


def _rms_cast(x, g, tm):
    t = x.shape[0]
    return pl.pallas_call(
        _rms_cast_kernel,
        grid=(t // tm,),
        in_specs=[pl.BlockSpec((tm, D_MODEL), lambda i: (i, 0)), pl.BlockSpec((1, D_MODEL), lambda i: (0, 0))],
        out_specs=pl.BlockSpec((tm, D_MODEL), lambda i: (i, 0)),
        out_shape=jax.ShapeDtypeStruct((t, D_MODEL), BF16),
        compiler_params=_params(("arbitrary",)),
        name="rms_cast",
    )(x, g)


def _glu_kernel(h_ref, wv_ref, wg_ref, u_ref):
    h = h_ref[...]
    u_ref[...] = jnp.dot(h, wv_ref[...], preferred_element_type=F32) * _sigmoid(
        jnp.dot(h, wg_ref[...], preferred_element_type=F32))


def _qkv_kernel(h_ref, w_ref, o_ref):
    scale = jnp.where(pl.program_id(1) == 0, HEAD_DIM ** -0.5, 1.0)
    o_ref[...] = jnp.dot(h_ref[...], w_ref[...], preferred_element_type=F32) * scale


def _gates_kernel(h_ref, w_ref, o_ref):
    o_ref[...] = _sigmoid(jnp.dot(h_ref[...], w_ref[...], preferred_element_type=F32))


def _proj(kern, name, h, w_bf, tm, tn, n_col, col_tile0):
    t = h.shape[0]
    w_specs = [pl.BlockSpec((D_MODEL, tn), functools.partial(lambda c0, i, j: (0, c0 + j), c0))
               for c0 in col_tile0]
    return pl.pallas_call(
        kern,
        grid=(t // tm, n_col),
        in_specs=[pl.BlockSpec((tm, D_MODEL), lambda i, j: (i, 0))] + w_specs,
        out_specs=pl.BlockSpec((tm, tn), lambda i, j: (i, j)),
        out_shape=jax.ShapeDtypeStruct((t, tn * n_col), F32),
        compiler_params=_params(("arbitrary", "arbitrary")),
        name=name,
    )(h, *([w_bf] * len(col_tile0)))


def _qkv_prompt_kernel(h_ref, wq_ref, wkt_ref, wv_ref, q_ref, kt_ref, vf_ref):
    h = h_ref[...]
    tm = h.shape[0]
    q_ref[...] = jnp.dot(h, wq_ref[...], preferred_element_type=F32) * (HEAD_DIM ** -0.5)
    kt_ref[...] = lax.dot_general(wkt_ref[...], h, NT_DIMS, preferred_element_type=F32)
    v = jnp.dot(h, wv_ref[...], preferred_element_type=F32)
    for hd in range(N_HEADS):
        vf_ref[pl.ds(hd, tm, stride=N_HEADS), :] = v[:, hd * V_DIM:(hd + 1) * V_DIM]


def _qkv_prompt(h, w_bf, seq, tm):
    t = h.shape[0]
    width = N_HEADS * V_DIM
    tiles_per_seq = seq // tm
    q0 = 2 * D_CONV // width
    wkt = w_bf[:, (q0 + 1) * width:(q0 + 2) * width].T
    fixed = lambda c: pl.BlockSpec((D_MODEL, width), functools.partial(lambda c, i: (0, c), c),
                                   pipeline_mode=pl.Buffered(1))
    return pl.pallas_call(
        _qkv_prompt_kernel,
        grid=(t // tm,),
        in_specs=[
            pl.BlockSpec((tm, D_MODEL), lambda i: (i, 0)),
            fixed(q0),
            pl.BlockSpec((width, D_MODEL), lambda i: (0, 0), pipeline_mode=pl.Buffered(1)),
            fixed(q0 + 2),
        ],
        out_specs=[
            pl.BlockSpec((tm, width), lambda i: (i, 0)),
            pl.BlockSpec((width, tm), lambda i: (i // tiles_per_seq, i % tiles_per_seq)),
            pl.BlockSpec((tm * N_HEADS, V_DIM), lambda i: (i, 0)),
        ],
        out_shape=[
            jax.ShapeDtypeStruct((t, width), F32),
            jax.ShapeDtypeStruct((t // seq * width, seq), F32),
            jax.ShapeDtypeStruct((t * N_HEADS, V_DIM), F32),
        ],
        compiler_params=_params(("arbitrary",)),
        name="proj_qkv_prompt",
    )(h, w_bf, wkt, w_bf)


def _in_proj(x, g, w_bf, tm, tm_wide, prompt_seq=None):
    h = _rms_cast(x, g, tm)
    glu_tn = 512
    u = _proj(_glu_kernel, "proj_glu", h, w_bf, tm_wide, glu_tn, D_CONV // glu_tn, [0, D_CONV // glu_tn])
    tn = PROJ_TN
    qkv0 = 2 * D_CONV // tn
    if prompt_seq is None:
        qkv = _proj(_qkv_kernel, "proj_qkv", h, w_bf, tm, tn, 3, [qkv0])
    else:
        qkv = _qkv_prompt(h, w_bf, prompt_seq, tm)
    gates = _proj(_gates_kernel, "proj_gates", h, w_bf, tm_wide, tn, 2 * D_MODEL // tn, [qkv0 + 3])
    return u, qkv, gates


def _ln_swish(c, g, b):
    mu = jnp.mean(c, axis=-1, keepdims=True)
    cc = c - mu
    y = cc * lax.rsqrt(jnp.mean(cc * cc, axis=-1, keepdims=True) + EPS) * g + b
    return y * _sigmoid(y)


def _conv_prompt_kernel(tiles_per_seq, rows_chunk, u_ref, halo_ref, w_ref, b_ref, g_ref, beta_ref,
                        o_ref, ext_sc, c_sc, wb_sc):
    i = pl.program_id(0)
    tm = u_ref.shape[0]

    @pl.when(i == 0)
    def _():
        for tap in range(CONV_WIDTH):
            wb_sc[tap] = jnp.broadcast_to(w_ref[tap:tap + 1, :], (8, D_CONV))
        wb_sc[CONV_WIDTH] = jnp.broadcast_to(b_ref[...], (8, D_CONV))

    first = (i % tiles_per_seq) == 0
    ext_sc[0:CONV_HALO, :] = jnp.where(first, 0.0, halo_ref[...])
    ext_sc[CONV_HALO:, :] = u_ref[...]
    lead = CONV_HALO - (CONV_WIDTH - 1)
    col_chunk = 256
    groups = rows_chunk // 8

    def body(r, carry):
        r0 = pl.multiple_of(r * rows_chunk, rows_chunk)
        for c0 in range(0, D_CONV, col_chunk):
            cols = slice(c0, c0 + col_chunk)
            accs = [wb_sc[CONV_WIDTH, :, cols]] * groups
            win = ext_sc[pl.ds(r0, rows_chunk + CONV_HALO), cols]
            for rem in range(8):
                shifted = win if rem == 0 else win[rem:rem + rows_chunk + CONV_HALO - 8]
                for tap in range(CONV_WIDTH):
                    if (lead + tap) % 8 != rem:
                        continue
                    a0 = (lead + tap) - rem
                    w_tap = wb_sc[tap, :, cols]
                    accs = [accs[gi] + w_tap * shifted[a0 + 8 * gi:a0 + 8 * gi + 8] for gi in range(groups)]
            for gi in range(groups):
                c_sc[pl.ds(r0 + 8 * gi, 8), cols] = accs[gi]
        return carry

    lax.fori_loop(0, tm // rows_chunk, body, 0)
    o_ref[...] = _ln_swish(c_sc[...], g_ref[...], beta_ref[...]).astype(o_ref.dtype)


def _conv_prompt(u, w, b, g, beta, seq, tm):
    t = u.shape[0]
    tiles_per_seq = seq // tm
    halo_per_tile = tm // CONV_HALO
    vec = lambda i: (0, 0)
    return pl.pallas_call(
        functools.partial(_conv_prompt_kernel, tiles_per_seq, 32),
        grid=(t // tm,),
        in_specs=[
            pl.BlockSpec((tm, D_CONV), lambda i: (i, 0)),
            pl.BlockSpec((CONV_HALO, D_CONV), lambda i: (jnp.maximum(i * halo_per_tile - 1, 0), 0)),
            pl.BlockSpec((CONV_WIDTH, D_CONV), vec),
            pl.BlockSpec((1, D_CONV), vec),
            pl.BlockSpec((1, D_CONV), vec),
            pl.BlockSpec((1, D_CONV), vec),
        ],
        out_specs=pl.BlockSpec((tm, D_CONV), lambda i: (i, 0)),
        out_shape=jax.ShapeDtypeStruct((t, D_CONV), BF16),
        scratch_shapes=[pltpu.VMEM((tm + CONV_HALO, D_CONV), F32), pltpu.VMEM((tm, D_CONV), F32),
                        pltpu.VMEM((CONV_WIDTH + 1, 8, D_CONV), F32)],
        compiler_params=_params(("arbitrary",)),
        name="conv_prompt",
    )(u, u, w, b, g, beta)


def _conv_sample_kernel(u_ref, st_ref, w_ref, b_ref, g_ref, beta_ref, o_ref, new_ref, ext_sc):
    nb, dec, _ = u_ref.shape
    hist = CONV_WIDTH - 1
    ext_sc[:, 0:hist, :] = st_ref[...]
    ext_sc[:, hist:hist + dec, :] = u_ref[...]
    acc = jnp.broadcast_to(b_ref[...][None], (nb, dec, D_CONV))
    for tap in range(CONV_WIDTH):
        acc = acc + w_ref[tap:tap + 1, :][None] * ext_sc[:, tap:tap + dec, :]
    o_ref[...] = _ln_swish(acc, g_ref[...][None], beta_ref[...][None]).astype(o_ref.dtype)
    new_ref[...] = ext_sc[:, dec:dec + hist, :]


def _conv_sample(u3, state, w, b, g, beta):
    nb, dec, _ = u3.shape
    hist = CONV_WIDTH - 1
    return pl.pallas_call(
        _conv_sample_kernel,
        out_shape=[
            jax.ShapeDtypeStruct((nb, dec, D_CONV), BF16),
            jax.ShapeDtypeStruct((nb, hist, D_CONV), F32),
        ],
        scratch_shapes=[pltpu.VMEM((nb, hist + dec, D_CONV), F32)],
        compiler_params=pltpu.CompilerParams(vmem_limit_bytes=VMEM_LIMIT_BYTES),
        name="conv_sample",
    )(u3, state, w, b, g, beta)


def _lambda(lq1, lk1, lq2, lk2):
    return (jnp.exp(jnp.sum(lq1 * lk1, axis=-1, keepdims=True))
            - jnp.exp(jnp.sum(lq2 * lk2, axis=-1, keepdims=True)) + LAM_INIT)


def _stack_maps(q):
    first = lax.broadcasted_iota(jnp.int32, q.shape, 1) < HEAD_DIM
    zero = jnp.zeros_like(q)
    return jnp.concatenate([jnp.where(first, q, zero), jnp.where(first, zero, q)], axis=0)


def _attn_prompt_kernel(slopes_ref, q_ref, kt_ref, v_ref, lq1, lk1, lq2, lk2, gcol_ref, o_ref,
                        kb_sc, vt_sc, q2_sc, bias_sc, bdiag_sc, s_sc, p_sc, alpha_sc, m_sc, l_sc, acc_sc):
    h = pl.program_id(1)
    qi = pl.program_id(2)
    tq = q_ref.shape[0]
    slope = slopes_ref[h]

    @pl.when(qi == 0)
    def _():
        kb_sc[...] = kt_ref[...].T.astype(BF16)
        for kb in range(kt_ref.shape[1] // tq):
            vt_sc[kb] = v_ref[pl.ds(kb * tq * N_HEADS + h, tq, stride=N_HEADS), :].T.astype(BF16)
        key = lax.broadcasted_iota(jnp.int32, (tq, 2 * tq), 0)
        qry = lax.broadcasted_iota(jnp.int32, (tq, 2 * tq), 1)
        dist = (jnp.where(qry >= tq, qry - tq, qry) - key).astype(F32)
        bias_sc[...] = slope * dist
        bdiag_sc[...] = jnp.where(dist >= 0.0, slope * dist, -NEG_INF)

    q2_sc[...] = _stack_maps(q_ref[...]).astype(BF16)
    m_sc[...] = jnp.full_like(m_sc, -jnp.inf)
    l_sc[...] = jnp.zeros_like(l_sc)
    acc_sc[...] = jnp.zeros_like(acc_sc)

    def scores(kb):
        start = pl.multiple_of(kb * tq, tq)
        return lax.dot_general(kb_sc[pl.ds(start, tq), :], q2_sc[...], NT_DIMS, preferred_element_type=F32)

    def weighted_values(kb):
        acc_sc[...] = alpha_sc[...] * acc_sc[...] + jnp.dot(vt_sc[kb], p_sc[...], preferred_element_type=F32)

    def softmax_step(s_raw, bias_ref, beta):
        s = s_raw - bias_ref[...]
        m_prev = m_sc[...]
        m_new = jnp.maximum(m_prev, jnp.max(s, axis=0, keepdims=True) - beta)
        alpha = jnp.exp(m_prev - m_new)
        p = jnp.exp(s - (m_new + beta))
        l_sc[...] = alpha * l_sc[...] + jnp.sum(p, axis=0, keepdims=True)
        m_sc[...] = m_new
        alpha_sc[...] = alpha
        p_sc[...] = p.astype(BF16)

    s_sc[...] = scores(0)
    p_sc[...] = jnp.zeros_like(p_sc)
    alpha_sc[...] = jnp.ones_like(alpha_sc)

    def body(kb, carry):
        s_raw = s_sc[...]
        s_sc[...] = scores(kb + 1)
        weighted_values(jnp.maximum(kb - 1, 0))
        softmax_step(s_raw, bias_sc, slope * ((qi - kb) * tq).astype(F32))
        return carry

    lax.fori_loop(0, qi, body, 0)
    weighted_values(jnp.maximum(qi - 1, 0))
    softmax_step(s_sc[...], bdiag_sc, 0.0)
    weighted_values(qi)

    lam = _lambda(lq1[...], lk1[...], lq2[...], lk2[...])
    o = acc_sc[...] / l_sc[...]
    od = o[:, 0:tq] - lam * o[:, tq:2 * tq]
    od = od * lax.rsqrt(jnp.mean(od * od, axis=0, keepdims=True) + EPS) * gcol_ref[...] * (1.0 - LAM_INIT)
    o_ref[...] = od.T.astype(o_ref.dtype)


def _attn_prompt(q, kt, vf, slopes, lams, subln_g, batch, seq):
    tq = ATT_TQ
    nq = seq // tq
    vec = lambda b, h, i: (0, 0)
    return pl.pallas_call(
        _attn_prompt_kernel,
        grid=(batch, N_HEADS, nq),
        in_specs=[
            pl.BlockSpec(memory_space=pltpu.SMEM),
            pl.BlockSpec((tq, V_DIM), lambda b, h, i: (b * nq + i, h)),
            pl.BlockSpec((V_DIM, seq), lambda b, h, i: (b * N_HEADS + h, 0)),
            pl.BlockSpec((seq * N_HEADS, V_DIM), lambda b, h, i: (b, 0)),
            pl.BlockSpec((1, HEAD_DIM), vec),
            pl.BlockSpec((1, HEAD_DIM), vec),
            pl.BlockSpec((1, HEAD_DIM), vec),
            pl.BlockSpec((1, HEAD_DIM), vec),
            pl.BlockSpec((V_DIM, 1), vec),
        ],
        out_specs=pl.BlockSpec((tq, V_DIM), lambda b, h, i: (b * nq + i, h)),
        out_shape=jax.ShapeDtypeStruct((batch * seq, N_HEADS * V_DIM), BF16),
        scratch_shapes=[
            pltpu.VMEM((seq, V_DIM), BF16),
            pltpu.VMEM((nq, V_DIM, tq), BF16),
            pltpu.VMEM((2 * tq, V_DIM), BF16),
            pltpu.VMEM((tq, 2 * tq), F32),
            pltpu.VMEM((tq, 2 * tq), F32),
            pltpu.VMEM((tq, 2 * tq), F32),
            pltpu.VMEM((tq, 2 * tq), BF16),
            pltpu.VMEM((1, 2 * tq), F32),
            pltpu.VMEM((1, 2 * tq), F32),
            pltpu.VMEM((1, 2 * tq), F32),
            pltpu.VMEM((V_DIM, 2 * tq), F32),
        ],
        compiler_params=_params(("arbitrary", "arbitrary", "arbitrary")),
        name="attn_prompt",
    )(slopes, q, kt, vf, *lams, subln_g.reshape(V_DIM, 1))


def _attn_sample_kernel(past_len, pt_ref, q_ref, kn_ref, vn_ref, *rest):
    npg = PAGES_PER_STEP
    k_refs = rest[0:npg]
    v_refs = rest[npg:2 * npg]
    lq1, lk1, lq2, lk2, g_ref, o_ref, kt_sc, vb_sc, qbd_sc, sb_sc, m_sc, l_sc, acc_sc = rest[2 * npg:]
    j = pl.program_id(1)
    dec = q_ref.shape[1]
    width = q_ref.shape[2]
    rows = 2 * N_HEADS * dec
    rows_per_head = 2 * dec
    keys = npg * PAGE_SIZE
    chunk = 256
    rows_per_chunk = (chunk // HEAD_DIM) * dec

    row1 = lax.broadcasted_iota(jnp.int32, (rows, 1), 0)
    slope = jnp.zeros((rows, 1), F32)
    for h in range(N_HEADS):
        slope = jnp.where(row1 // rows_per_head == h, 2.0 ** -(h + 1), slope)

    @pl.when(j == 0)
    def _():
        m_sc[...] = jnp.full_like(m_sc, -jnp.inf)
        l_sc[...] = jnp.zeros_like(l_sc)
        acc_sc[...] = jnp.zeros_like(acc_sc)
        qt = jnp.concatenate([q_ref[0]] * (rows // dec), axis=0)
        rr = lax.broadcasted_iota(jnp.int32, (rows, width), 0)
        cc = lax.broadcasted_iota(jnp.int32, (rows, width), 1)
        qbd_sc[...] = jnp.where(cc // HEAD_DIM == rr // dec, qt, 0.0).astype(BF16)
        qpos = lax.broadcasted_iota(jnp.int32, (rows, keys), 0) % dec
        col = lax.broadcasted_iota(jnp.int32, (rows, keys), 1)
        sb_sc[...] = slope * (past_len + qpos - col).astype(F32)

    for r in range(npg):
        kt_sc[:, r * PAGE_SIZE:(r + 1) * PAGE_SIZE] = k_refs[r][0].astype(BF16)
        for h in range(N_HEADS):
            vb_sc[h, r * PAGE_SIZE:(r + 1) * PAGE_SIZE, :] = v_refs[r][
                0, pl.ds(h, PAGE_SIZE, stride=N_HEADS), :].astype(BF16)

    gamma = slope * (j * keys).astype(F32)

    def update(r0, nrows, s, g, pv_fn):
        sl = slice(r0, r0 + nrows)
        m_prev = m_sc[sl]
        m_new = jnp.maximum(m_prev, jnp.max(s, axis=-1, keepdims=True) + g)
        alpha = jnp.exp(m_prev - m_new)
        p = jnp.exp(s - (m_new - g))
        l_sc[sl] = alpha * l_sc[sl] + jnp.sum(p, axis=-1, keepdims=True)
        acc_sc[sl] = alpha * acc_sc[sl] + pv_fn(p.astype(BF16))
        m_sc[sl] = m_new

    n_chunks = width // chunk
    heads_per_chunk = rows_per_chunk // rows_per_head
    scores = [jnp.dot(qbd_sc[c * rows_per_chunk:(c + 1) * rows_per_chunk, c * chunk:(c + 1) * chunk],
                      kt_sc[c * chunk:(c + 1) * chunk, :], preferred_element_type=F32) for c in range(n_chunks)]
    probs, alphas = [], []
    for c in range(n_chunks):
        sl = slice(c * rows_per_chunk, (c + 1) * rows_per_chunk)
        s = scores[c] - sb_sc[sl, :]
        g = gamma[sl]
        m_prev = m_sc[sl]
        m_new = jnp.maximum(m_prev, jnp.max(s, axis=-1, keepdims=True) + g)
        alpha = jnp.exp(m_prev - m_new)
        p = jnp.exp(s - (m_new - g))
        l_sc[sl] = alpha * l_sc[sl] + jnp.sum(p, axis=-1, keepdims=True)
        m_sc[sl] = m_new
        probs.append(p.astype(BF16))
        alphas.append(alpha)
    for c in range(n_chunks):
        sl = slice(c * rows_per_chunk, (c + 1) * rows_per_chunk)
        pv = jnp.concatenate(
            [jnp.dot(probs[c][i * rows_per_head:(i + 1) * rows_per_head], vb_sc[c * heads_per_chunk + i],
                     preferred_element_type=F32) for i in range(heads_per_chunk)], axis=0)
        acc_sc[sl] = alphas[c] * acc_sc[sl] + pv

    @pl.when(j == pl.num_programs(1) - 1)
    def _():
        fill = jnp.zeros((PAGE_SIZE - dec, width), F32)
        k_new = jnp.concatenate([kn_ref[0], fill], axis=0).astype(BF16)
        v_new = jnp.concatenate([vn_ref[0], fill], axis=0).astype(BF16)
        qpos = lax.broadcasted_iota(jnp.int32, (rows, PAGE_SIZE), 0) % dec
        col = lax.broadcasted_iota(jnp.int32, (rows, PAGE_SIZE), 1)
        dist = (qpos - col).astype(F32)
        s = lax.dot_general(qbd_sc[...], k_new, NT_DIMS, preferred_element_type=F32)
        s = jnp.where(dist >= 0.0, s - slope * dist, NEG_INF)

        def pv(pb):
            return jnp.concatenate(
                [jnp.dot(pb[h * rows_per_head:(h + 1) * rows_per_head], v_new[:, h * V_DIM:(h + 1) * V_DIM],
                         preferred_element_type=F32) for h in range(N_HEADS)], axis=0)

        update(0, rows, s, 0.0, pv)

        lam = _lambda(lq1[...], lk1[...], lq2[...], lk2[...])
        o = acc_sc[...] / l_sc[...]
        for h in range(N_HEADS):
            r0 = h * rows_per_head
            od = o[r0:r0 + dec] - lam * o[r0 + dec:r0 + 2 * dec]
            o_ref[0, :, h * V_DIM:(h + 1) * V_DIM] = (_rms(od, g_ref[...]) * (1.0 - LAM_INIT)).astype(o_ref.dtype)


def _attn_sample(q3, k3, v3, col0, cache_kt, cache_vf, page_table, lams, subln_g, past_len):
    nb, dec, _ = q3.shape
    width = N_HEADS * V_DIM
    npg = PAGES_PER_STEP
    n_pages = page_table.shape[1]
    rows = 2 * N_HEADS * dec
    keys = npg * PAGE_SIZE
    tok = pl.BlockSpec((1, dec, width), lambda b, j, pt: (b, 0, 0))
    toks = [pl.BlockSpec((1, dec, width), functools.partial(lambda c, b, j, pt: (b, 0, c), c)) for c in col0]
    vec = lambda b, j, pt: (0, 0)

    def page_spec(r, shape):
        return pl.BlockSpec((1,) + shape, lambda b, j, pt: (pt[b * n_pages + j * npg + r], 0, 0))

    grid_spec = pltpu.PrefetchScalarGridSpec(
        num_scalar_prefetch=1,
        grid=(nb, n_pages // npg),
        in_specs=toks
        + [page_spec(r, cache_kt.shape[1:]) for r in range(npg)]
        + [page_spec(r, cache_vf.shape[1:]) for r in range(npg)]
        + [pl.BlockSpec((1, HEAD_DIM), vec)] * 4
        + [pl.BlockSpec((1, V_DIM), vec)],
        out_specs=tok,
        scratch_shapes=[
            pltpu.VMEM((width, keys), BF16),
            pltpu.VMEM((N_HEADS, keys, V_DIM), BF16),
            pltpu.VMEM((rows, width), BF16),
            pltpu.VMEM((rows, keys), F32),
            pltpu.VMEM((rows, 1), F32),
            pltpu.VMEM((rows, 1), F32),
            pltpu.VMEM((rows, V_DIM), F32),
        ],
    )
    return pl.pallas_call(
        functools.partial(_attn_sample_kernel, past_len),
        grid_spec=grid_spec,
        out_shape=jax.ShapeDtypeStruct((nb, dec, width), F32),
        compiler_params=_params(("arbitrary", "arbitrary")),
        name="attn_sample",
    )(page_table.reshape(-1), q3, k3, v3, *([cache_kt] * npg), *([cache_vf] * npg), *lams, subln_g)


def _merge_kernel(c_ref, od_ref, ga_ref, gb_ref, x_ref, wco_ref, wao_ref, wo_ref, g_ref, x1_ref, h2_ref):
    out_a = jnp.dot(c_ref[...].astype(BF16), wco_ref[...], preferred_element_type=F32)
    out_b = jnp.dot(od_ref[...].astype(BF16), wao_ref[...], preferred_element_type=F32)
    merged = ga_ref[...] * out_a + gb_ref[...] * out_b
    x1 = x_ref[...] + jnp.dot(merged.astype(BF16), wo_ref[...], preferred_element_type=F32)
    x1_ref[...] = x1
    h2_ref[...] = _rms(x1, g_ref[...]).astype(BF16)


def _merge(c_act, od, gates, x, wco, wao, wo, g, tm):
    t = x.shape[0]
    row = lambda i: (i, 0)
    fixed = lambda i: (0, 0)
    resident = functools.partial(pl.BlockSpec, index_map=fixed, pipeline_mode=pl.Buffered(1))
    return pl.pallas_call(
        _merge_kernel,
        grid=(t // tm,),
        in_specs=[
            pl.BlockSpec((tm, D_CONV), row),
            pl.BlockSpec((tm, N_HEADS * V_DIM), row),
            pl.BlockSpec((tm, D_MODEL), row),
            pl.BlockSpec((tm, D_MODEL), lambda i: (i, 1)),
            pl.BlockSpec((tm, D_MODEL), row),
            resident((D_CONV, D_MODEL)),
            resident((N_HEADS * V_DIM, D_MODEL)),
            resident((D_MODEL, D_MODEL)),
            pl.BlockSpec((1, D_MODEL), fixed),
        ],
        out_specs=[pl.BlockSpec((tm, D_MODEL), row), pl.BlockSpec((tm, D_MODEL), row)],
        out_shape=[jax.ShapeDtypeStruct((t, D_MODEL), F32), jax.ShapeDtypeStruct((t, D_MODEL), BF16)],
        compiler_params=_params(("arbitrary",)),
        name="merge",
    )(c_act, od, gates, gates, x, wco, wao, wo, g)


def _ffn_act(gc, h2, wv_ref):
    val = jnp.dot(h2, wv_ref[...], preferred_element_type=F32)
    return (gc * _sigmoid(gc) * val).astype(BF16)


def _ffn_up_prompt_kernel(tiles_per_seq, h2_ref, halo_ref, wg_ref, wv_ref, cw_ref, cb_ref, act_ref, tail_ref,
                          gext_sc):
    i = pl.program_id(0)
    tm = h2_ref.shape[0]
    h2 = h2_ref[...]
    gate = jnp.dot(h2, wg_ref[...], preferred_element_type=F32)
    gate_halo = jnp.dot(halo_ref[...], wg_ref[...], preferred_element_type=F32)
    first = (i % tiles_per_seq) == 0
    gext_sc[0:FFN_HALO, :] = jnp.where(first, 0.0, gate_halo)
    gext_sc[FFN_HALO:, :] = gate
    tail_ref[0] = gate[tm - 8:tm, :]
    gc = (cw_ref[0:1, :] * gext_sc[pl.ds(FFN_HALO - 2, tm), :]
          + cw_ref[1:2, :] * gext_sc[pl.ds(FFN_HALO - 1, tm), :]
          + cw_ref[2:3, :] * gate + cb_ref[...])
    act_ref[...] = _ffn_act(gc, h2, wv_ref)


def _ffn_up_sample_kernel(dec, h2_ref, e0_ref, e1_ref, wg_ref, wv_ref, cw_ref, cb_ref, act_ref, gate_ref):
    h2 = h2_ref[...]
    gate = jnp.dot(h2, wg_ref[...], preferred_element_type=F32)
    gate_ref[...] = gate
    pos = lax.broadcasted_iota(jnp.int32, gate.shape, 0) % dec
    g1 = jnp.where(pos == 0, e1_ref[...], pltpu.roll(gate, 1, 0))
    g2 = jnp.where(pos <= 1, e0_ref[...], pltpu.roll(gate, 2, 0))
    gc = cw_ref[0:1, :] * g2 + cw_ref[1:2, :] * g1 + cw_ref[2:3, :] * gate + cb_ref[...]
    act_ref[...] = _ffn_act(gc, h2, wv_ref)


def _ffn_up_specs(tm, tf):
    nf = D_FF // tf
    return dict(
        h2=pl.BlockSpec((tm, D_MODEL), lambda i, f: (i, 0)),
        wg=pl.BlockSpec((D_MODEL, tf), lambda i, f: (0, f)),
        wv=pl.BlockSpec((D_MODEL, tf), lambda i, f: (0, nf + f)),
        cw=pl.BlockSpec((FFN_CONV_WIDTH, tf), lambda i, f: (0, f)),
        cb=pl.BlockSpec((1, tf), lambda i, f: (0, f)),
        act=pl.BlockSpec((tm, tf), lambda i, f: (i, f)),
    )


def _ffn_up_prompt(h2, w_in_bf, cw, cb, seq, tm):
    t = h2.shape[0]
    tf = FFN_TF
    sp = _ffn_up_specs(tm, tf)
    halo_per_tile = tm // FFN_HALO
    return pl.pallas_call(
        functools.partial(_ffn_up_prompt_kernel, seq // tm),
        grid=(t // tm, D_FF // tf),
        in_specs=[
            sp["h2"],
            pl.BlockSpec((FFN_HALO, D_MODEL), lambda i, f: (jnp.maximum(i * halo_per_tile - 1, 0), 0)),
            sp["wg"], sp["wv"], sp["cw"], sp["cb"],
        ],
        out_specs=[sp["act"], pl.BlockSpec((1, 8, tf), lambda i, f: (i, 0, f))],
        out_shape=[jax.ShapeDtypeStruct((t, D_FF), BF16), jax.ShapeDtypeStruct((t // tm, 8, D_FF), F32)],
        scratch_shapes=[pltpu.VMEM((tm + FFN_HALO, tf), F32)],
        compiler_params=_params(("arbitrary", "arbitrary")),
        name="ffn_up_prompt",
    )(h2, h2, w_in_bf, w_in_bf, cw, cb)


def _ffn_up_sample(h2, e0, e1, w_in_bf, cw, cb, dec):
    t = h2.shape[0]
    tf = FFN_TF
    sp = _ffn_up_specs(t, tf)
    hist = pl.BlockSpec((t, tf), lambda i, f: (0, f))
    return pl.pallas_call(
        functools.partial(_ffn_up_sample_kernel, dec),
        grid=(1, D_FF // tf),
        in_specs=[sp["h2"], hist, hist, sp["wg"], sp["wv"], sp["cw"], sp["cb"]],
        out_specs=[sp["act"], hist],
        out_shape=[jax.ShapeDtypeStruct((t, D_FF), BF16), jax.ShapeDtypeStruct((t, D_FF), F32)],
        compiler_params=_params(("arbitrary", "arbitrary")),
        name="ffn_up_sample",
    )(h2, e0, e1, w_in_bf, w_in_bf, cw, cb)


def _ffn_down_kernel(act_ref, w_ref, x1_ref, g_ref, y_ref):
    y_ref[...] = _rms(x1_ref[...] + jnp.dot(act_ref[...], w_ref[...], preferred_element_type=F32), g_ref[...])


def _ffn_down(act, x1, w_out_bf, gfin, tm):
    t = act.shape[0]
    row = lambda i: (i, 0)
    fixed = lambda i: (0, 0)
    return pl.pallas_call(
        _ffn_down_kernel,
        grid=(t // tm,),
        in_specs=[
            pl.BlockSpec((tm, D_FF), row),
            pl.BlockSpec((D_FF, D_MODEL), fixed, pipeline_mode=pl.Buffered(1)),
            pl.BlockSpec((tm, D_MODEL), row),
            pl.BlockSpec((1, D_MODEL), fixed),
        ],
        out_specs=pl.BlockSpec((tm, D_MODEL), row),
        out_shape=jax.ShapeDtypeStruct((t, D_MODEL), F32),
        compiler_params=_params(("arbitrary",)),
        name="ffn_down",
    )(act, w_out_bf, x1, gfin)


def kernel(x_prompt, x_sample, cache_k, cache_v, state_conv, state_ffn, page_table, norm_mix_g, w_in, conv_w,
           conv_b, conv_ln_g, conv_ln_b, w_conv_out, lambda_q1, lambda_k1, lambda_q2, lambda_k2, subln_g,
           w_attn_out, w_o, norm_ffn_g, w_ffn_in, ffn_conv_w, ffn_conv_b, w_ffn_out, norm_final_g):
    batch, seq, _ = x_prompt.shape
    nb, dec, _ = x_sample.shape
    n_pool = cache_k.shape[1]
    past_len = page_table.shape[1] * PAGE_SIZE
    width = N_HEADS * V_DIM
    layer = 0

    row = lambda a: a[layer].reshape(1, -1)
    w_in_bf = w_in[layer].astype(BF16)
    wco_bf = w_conv_out[layer].astype(BF16)
    wao_bf = w_attn_out[layer].astype(BF16)
    wo_bf = w_o[layer].astype(BF16)
    wfi_bf = w_ffn_in[layer].astype(BF16)
    wfo_bf = w_ffn_out[layer].astype(BF16)
    g_mix, g_ffn, g_fin = row(norm_mix_g), row(norm_ffn_g), norm_final_g.reshape(1, -1)
    cw, cb, lng, lnb = conv_w[layer], row(conv_b), row(conv_ln_g), row(conv_ln_b)
    fcw, fcb = ffn_conv_w[layer], row(ffn_conv_b)
    lams = (row(lambda_q1), row(lambda_k1), row(lambda_q2), row(lambda_k2))
    sub_g = row(subln_g)
    slopes = jnp.asarray([2.0 ** -(h + 1) for h in range(N_HEADS)], F32)

    xp = x_prompt.reshape(batch * seq, D_MODEL)
    u_p, (q_p, kt_p, vf_p), gates_p = _in_proj(xp, g_mix, w_in_bf, 512, 1024, prompt_seq=seq)
    c_p = _conv_prompt(u_p, cw, cb, lng, lnb, seq, 512)
    od_p = _attn_prompt(q_p, kt_p, vf_p, slopes, lams, sub_g, batch, seq)
    x1_p, h2_p = _merge(c_p, od_p, gates_p, xp, wco_bf, wao_bf, wo_bf, g_ffn, 256)
    ffn_tm = 1024
    act_p, tail_p = _ffn_up_prompt(h2_p, wfi_bf, fcw, fcb, seq, ffn_tm)
    y_p = _ffn_down(act_p, x1_p, wfo_bf, g_fin, 256)

    xs = x_sample.reshape(nb * dec, D_MODEL)
    u_s, qkv_s, gates_s = _in_proj(xs, g_mix, w_in_bf, nb * dec, nb * dec)
    c_s, conv_s = _conv_sample(u_s.reshape(nb, dec, D_CONV), state_conv[layer], cw, cb, lng, lnb)
    cache_kt = jnp.transpose(cache_k[layer], (0, 2, 3, 4, 1)).reshape(n_pool, width, PAGE_SIZE)
    cache_vf = cache_v[layer].reshape(n_pool, PAGE_SIZE * N_HEADS, V_DIM)
    qkv_s3 = qkv_s.reshape(nb, dec, 3 * width)
    od_s = _attn_sample(qkv_s3, qkv_s3, qkv_s3, (0, 1, 2), cache_kt, cache_vf, page_table, lams, sub_g, past_len)
    x1_s, h2_s = _merge(c_s.reshape(nb * dec, D_CONV), od_s.reshape(nb * dec, width), gates_s, xs,
                        wco_bf, wao_bf, wo_bf, g_ffn, nb * dec)
    st = state_ffn[layer]
    pad = lambda a: jnp.pad(a, ((0, 0), (0, dec - a.shape[1]), (0, 0))).reshape(nb * dec, D_FF)
    e0 = pad(st)
    e1 = pad(st[:, 1:2])
    act_s, gate_s = _ffn_up_sample(h2_s, e0, e1, wfi_bf, fcw, fcb, dec)
    y_s = _ffn_down(act_s, x1_s, wfo_bf, g_fin, nb * dec)

    tiles_per_seq = seq // ffn_tm
    ffn_p = tail_p.reshape(batch, tiles_per_seq, 8, D_FF)[:, -1, 8 - (FFN_CONV_WIDTH - 1):, :]
    return (
        y_p.reshape(batch, seq, D_MODEL),
        y_s.reshape(nb, dec, D_MODEL),
        jnp.transpose(kt_p.reshape(1, batch, N_HEADS, 2, HEAD_DIM, seq), (0, 1, 5, 2, 3, 4)),
        vf_p.reshape(1, batch, seq, N_HEADS, V_DIM),
        u_p.reshape(batch, seq, D_CONV)[None, :, seq - (CONV_WIDTH - 1):, :],
        ffn_p[None],
        qkv_s[:, width:2 * width].reshape(1, nb, dec, N_HEADS, 2, HEAD_DIM),
        qkv_s[:, 2 * width:].reshape(1, nb, dec, N_HEADS, V_DIM),
        conv_s[None],
        gate_s.reshape(nb, dec, D_FF)[None, :, dec - (FFN_CONV_WIDTH - 1):, :],
    )
```

```python
import functools
import math

import jax
import jax.numpy as jnp
from jax import lax
from jax.experimental import pallas as pl
from jax.experimental.pallas import tpu as pltpu

D_MODEL = 2048
N_HEADS = 8
HEAD_DIM = 64
V_DIM = 2 * HEAD_DIM
D_CONV = D_MODEL // 2
CONV_WIDTH = 31
D_FF = 5632
FFN_CONV_WIDTH = 3
EPS = 1e-5
NEG_INF = -1e30
PAGE_SIZE = 128
LAM_INIT = 0.8 - 0.6 * math.exp(-0.3 * 0)

F32 = jnp.float32
BF16 = jnp.bfloat16

VMEM_LIMIT_BYTES = 56 * 1024 * 1024

PROJ_TN = 1024
ATT_TQ = 256
PAGES_PER_STEP = 16
FFN_TF = 512
FFN_HALO = 16
CONV_HALO = 32


NT_DIMS = (((1,), (1,)), ((), ()))


def _params(sem):
    return pltpu.CompilerParams(dimension_semantics=sem, vmem_limit_bytes=VMEM_LIMIT_BYTES)


def _sigmoid(x):
    return 1.0 / (1.0 + jnp.exp(-x))


def _rms(x, g):
    return x * lax.rsqrt(jnp.mean(x * x, axis=-1, keepdims=True) + EPS) * g


def _rms_cast_kernel(x_ref, g_ref, h_ref):
    h_ref[...] = _rms(x_ref[...], g_ref[...]).astype(BF16)


def _rms_cast(x, g, tm):
    t = x.shape[0]
    return pl.pallas_call(
        _rms_cast_kernel,
        grid=(t // tm,),
        in_specs=[pl.BlockSpec((tm, D_MODEL), lambda i: (i, 0)), pl.BlockSpec((1, D_MODEL), lambda i: (0, 0))],
        out_specs=pl.BlockSpec((tm, D_MODEL), lambda i: (i, 0)),
        out_shape=jax.ShapeDtypeStruct((t, D_MODEL), BF16),
        compiler_params=_params(("arbitrary",)),
        name="rms_cast",
    )(x, g)


def _glu_kernel(h_ref, wv_ref, wg_ref, u_ref):
    h = h_ref[...]
    u_ref[...] = jnp.dot(h, wv_ref[...], preferred_element_type=F32) * _sigmoid(
        jnp.dot(h, wg_ref[...], preferred_element_type=F32))


def _qkv_kernel(h_ref, w_ref, o_ref):
    scale = jnp.where(pl.program_id(1) == 0, HEAD_DIM ** -0.5, 1.0)
    o_ref[...] = jnp.dot(h_ref[...], w_ref[...], preferred_element_type=F32) * scale


def _gates_kernel(h_ref, w_ref, o_ref):
    o_ref[...] = _sigmoid(jnp.dot(h_ref[...], w_ref[...], preferred_element_type=F32))


def _with_cast(kern, n_w):
    def wrapped(h_ref, *refs):
        kern(h_ref, *refs[:n_w], refs[n_w + 1])
        refs[n_w + 2][...] = refs[n_w][...].astype(BF16)
    return wrapped


def _proj(kern, name, h, w_bf, tm, tn, n_col, col_tile0, cast=None):
    t = h.shape[0]
    w_specs = [pl.BlockSpec((D_MODEL, tn), functools.partial(lambda c0, i, j: (0, c0 + j), c0))
               for c0 in col_tile0]
    in_specs = [pl.BlockSpec((tm, D_MODEL), lambda i, j: (i, 0))] + w_specs
    out_specs = [pl.BlockSpec((tm, tn), lambda i, j: (i, j))]
    out_shape = [jax.ShapeDtypeStruct((t, tn * n_col), F32)]
    args = [h] + [w_bf] * len(col_tile0)
    if cast is not None:
        slab = cast.shape[0] // ((t // tm) * n_col)
        assert slab * (t // tm) * n_col == cast.shape[0] and slab % 16 == 0, cast.shape
        slab_spec = pl.BlockSpec((slab, cast.shape[1]), lambda i, j: (i * n_col + j, 0))
        in_specs.append(slab_spec)
        out_specs.append(slab_spec)
        out_shape.append(jax.ShapeDtypeStruct(cast.shape, BF16))
        args.append(cast)
        kern = _with_cast(kern, len(col_tile0))
    out = pl.pallas_call(
        kern,
        grid=(t // tm, n_col),
        in_specs=in_specs,
        out_specs=out_specs,
        out_shape=out_shape,
        compiler_params=_params(("arbitrary", "arbitrary")),
        name=name,
    )(*args)
    return out[0] if cast is None else out


def _qkv_prompt_kernel(h_ref, wq_ref, wkt_ref, wv_ref, q_ref, kt_ref, vf_ref):
    h = h_ref[...]
    tm = h.shape[0]
    q_ref[...] = jnp.dot(h, wq_ref[...], preferred_element_type=F32) * (HEAD_DIM ** -0.5)
    kt_ref[...] = lax.dot_general(wkt_ref[...], h, NT_DIMS, preferred_element_type=F32)
    v = jnp.dot(h, wv_ref[...], preferred_element_type=F32)
    for hd in range(N_HEADS):
        vf_ref[pl.ds(hd, tm, stride=N_HEADS), :] = v[:, hd * V_DIM:(hd + 1) * V_DIM]


def _qkv_prompt(h, w_bf, seq, tm):
    t = h.shape[0]
    width = N_HEADS * V_DIM
    tiles_per_seq = seq // tm
    q0 = 2 * D_CONV // width
    wkt = w_bf[:, (q0 + 1) * width:(q0 + 2) * width].T
    fixed = lambda c: pl.BlockSpec((D_MODEL, width), functools.partial(lambda c, i: (0, c), c),
                                   pipeline_mode=pl.Buffered(1))
    return pl.pallas_call(
        _qkv_prompt_kernel,
        grid=(t // tm,),
        in_specs=[
            pl.BlockSpec((tm, D_MODEL), lambda i: (i, 0)),
            fixed(q0),
            pl.BlockSpec((width, D_MODEL), lambda i: (0, 0), pipeline_mode=pl.Buffered(1)),
            fixed(q0 + 2),
        ],
        out_specs=[
            pl.BlockSpec((tm, width), lambda i: (i, 0)),
            pl.BlockSpec((width, tm), lambda i: (i // tiles_per_seq, i % tiles_per_seq)),
            pl.BlockSpec((tm * N_HEADS, V_DIM), lambda i: (i, 0)),
        ],
        out_shape=[
            jax.ShapeDtypeStruct((t, width), F32),
            jax.ShapeDtypeStruct((t // seq * width, seq), F32),
            jax.ShapeDtypeStruct((t * N_HEADS, V_DIM), F32),
        ],
        compiler_params=_params(("arbitrary",)),
        name="proj_qkv_prompt",
    )(h, w_bf, wkt, w_bf)


def _in_proj(x, g, w_bf, tm, tm_wide, prompt_seq=None, casts=(None, None)):
    h = _rms_cast(x, g, tm)
    glu_tn = 512
    u = _proj(_glu_kernel, "proj_glu", h, w_bf, tm_wide, glu_tn, D_CONV // glu_tn, [0, D_CONV // glu_tn],
              cast=casts[0])
    tn = PROJ_TN
    qkv0 = 2 * D_CONV // tn
    if prompt_seq is None:
        qkv = _proj(_qkv_kernel, "proj_qkv", h, w_bf, tm, tn, 3, [qkv0])
    else:
        qkv = _qkv_prompt(h, w_bf, prompt_seq, tm)
    gates = _proj(_gates_kernel, "proj_gates", h, w_bf, tm_wide, tn, 2 * D_MODEL // tn, [qkv0 + 3],
                  cast=casts[1])
    cast_out = (None if casts[0] is None else u[1], None if casts[1] is None else gates[1])
    if casts[0] is not None:
        u = u[0]
    if casts[1] is not None:
        gates = gates[0]
    return u, qkv, gates, cast_out


def _ln_swish(c, g, b):
    mu = jnp.mean(c, axis=-1, keepdims=True)
    cc = c - mu
    y = cc * lax.rsqrt(jnp.mean(cc * cc, axis=-1, keepdims=True) + EPS) * g + b
    return y * _sigmoid(y)


def _conv_prompt_kernel(tiles_per_seq, rows_chunk, u_ref, halo_ref, w_ref, b_ref, g_ref, beta_ref,
                        o_ref, ext_sc, c_sc, wb_sc):
    i = pl.program_id(0)
    tm = u_ref.shape[0]

    @pl.when(i == 0)
    def _():
        for tap in range(CONV_WIDTH):
            wb_sc[tap] = jnp.broadcast_to(w_ref[tap:tap + 1, :], (8, D_CONV))
        wb_sc[CONV_WIDTH] = jnp.broadcast_to(b_ref[...], (8, D_CONV))

    first = (i % tiles_per_seq) == 0
    ext_sc[0:CONV_HALO, :] = jnp.where(first, 0.0, halo_ref[...])
    ext_sc[CONV_HALO:, :] = u_ref[...]
    lead = CONV_HALO - (CONV_WIDTH - 1)
    col_chunk = 256
    groups = rows_chunk // 8

    def body(r, carry):
        r0 = pl.multiple_of(r * rows_chunk, rows_chunk)
        for c0 in range(0, D_CONV, col_chunk):
            cols = slice(c0, c0 + col_chunk)
            accs = [wb_sc[CONV_WIDTH, :, cols]] * groups
            win = ext_sc[pl.ds(r0, rows_chunk + CONV_HALO), cols]
            for rem in range(8):
                shifted = win if rem == 0 else win[rem:rem + rows_chunk + CONV_HALO - 8]
                for tap in range(CONV_WIDTH):
                    if (lead + tap) % 8 != rem:
                        continue
                    a0 = (lead + tap) - rem
                    w_tap = wb_sc[tap, :, cols]
                    accs = [accs[gi] + w_tap * shifted[a0 + 8 * gi:a0 + 8 * gi + 8] for gi in range(groups)]
            for gi in range(groups):
                c_sc[pl.ds(r0 + 8 * gi, 8), cols] = accs[gi]
        return carry

    lax.fori_loop(0, tm // rows_chunk, body, 0)
    o_ref[...] = _ln_swish(c_sc[...], g_ref[...], beta_ref[...]).astype(o_ref.dtype)


def _conv_prompt(u, w, b, g, beta, seq, tm):
    t = u.shape[0]
    tiles_per_seq = seq // tm
    halo_per_tile = tm // CONV_HALO
    vec = lambda i: (0, 0)
    return pl.pallas_call(
        functools.partial(_conv_prompt_kernel, tiles_per_seq, 32),
        grid=(t // tm,),
        in_specs=[
            pl.BlockSpec((tm, D_CONV), lambda i: (i, 0)),
            pl.BlockSpec((CONV_HALO, D_CONV), lambda i: (jnp.maximum(i * halo_per_tile - 1, 0), 0)),
            pl.BlockSpec((CONV_WIDTH, D_CONV), vec),
            pl.BlockSpec((1, D_CONV), vec),
            pl.BlockSpec((1, D_CONV), vec),
            pl.BlockSpec((1, D_CONV), vec),
        ],
        out_specs=pl.BlockSpec((tm, D_CONV), lambda i: (i, 0)),
        out_shape=jax.ShapeDtypeStruct((t, D_CONV), BF16),
        scratch_shapes=[pltpu.VMEM((tm + CONV_HALO, D_CONV), F32), pltpu.VMEM((tm, D_CONV), F32),
                        pltpu.VMEM((CONV_WIDTH + 1, 8, D_CONV), F32)],
        compiler_params=_params(("arbitrary",)),
        name="conv_prompt",
    )(u, u, w, b, g, beta)


def _conv_sample_kernel(u_ref, st_ref, w_ref, b_ref, g_ref, beta_ref, o_ref, new_ref, ext_sc):
    nb, dec, _ = u_ref.shape
    hist = CONV_WIDTH - 1
    ext_sc[:, 0:hist, :] = st_ref[...]
    ext_sc[:, hist:hist + dec, :] = u_ref[...]
    acc = jnp.broadcast_to(b_ref[...][None], (nb, dec, D_CONV))
    for tap in range(CONV_WIDTH):
        acc = acc + w_ref[tap:tap + 1, :][None] * ext_sc[:, tap:tap + dec, :]
    o_ref[...] = _ln_swish(acc, g_ref[...][None], beta_ref[...][None]).astype(o_ref.dtype)
    new_ref[...] = ext_sc[:, dec:dec + hist, :]


def _conv_sample(u3, state, w, b, g, beta):
    nb, dec, _ = u3.shape
    hist = CONV_WIDTH - 1
    return pl.pallas_call(
        _conv_sample_kernel,
        out_shape=[
            jax.ShapeDtypeStruct((nb, dec, D_CONV), BF16),
            jax.ShapeDtypeStruct((nb, hist, D_CONV), F32),
        ],
        scratch_shapes=[pltpu.VMEM((nb, hist + dec, D_CONV), F32)],
        compiler_params=pltpu.CompilerParams(vmem_limit_bytes=VMEM_LIMIT_BYTES),
        name="conv_sample",
    )(u3, state, w, b, g, beta)


def _lambda(lq1, lk1, lq2, lk2):
    return (jnp.exp(jnp.sum(lq1 * lk1, axis=-1, keepdims=True))
            - jnp.exp(jnp.sum(lq2 * lk2, axis=-1, keepdims=True)) + LAM_INIT)


def _stack_maps(q):
    first = lax.broadcasted_iota(jnp.int32, q.shape, 1) < HEAD_DIM
    zero = jnp.zeros_like(q)
    return jnp.concatenate([jnp.where(first, q, zero), jnp.where(first, zero, q)], axis=0)


def _attn_prompt_kernel(slopes_ref, q_ref, kt_ref, v_ref, lq1, lk1, lq2, lk2, gcol_ref, o_ref,
                        kb_sc, vt_sc, q2_sc, bias_sc, bdiag_sc, s_sc, p_sc, alpha_sc, m_sc, l_sc, acc_sc):
    h = pl.program_id(1)
    qi = pl.program_id(2)
    tq = q_ref.shape[0]
    slope = slopes_ref[h]

    @pl.when(qi == 0)
    def _():
        kb_sc[...] = kt_ref[...].T.astype(BF16)
        for kb in range(kt_ref.shape[1] // tq):
            vt_sc[kb] = v_ref[pl.ds(kb * tq * N_HEADS + h, tq, stride=N_HEADS), :].T.astype(BF16)
        key = lax.broadcasted_iota(jnp.int32, (tq, 2 * tq), 0)
        qry = lax.broadcasted_iota(jnp.int32, (tq, 2 * tq), 1)
        dist = (jnp.where(qry >= tq, qry - tq, qry) - key).astype(F32)
        bias_sc[...] = slope * dist
        bdiag_sc[...] = jnp.where(dist >= 0.0, slope * dist, -NEG_INF)

    q2_sc[...] = _stack_maps(q_ref[...]).astype(BF16)
    m_sc[...] = jnp.full_like(m_sc, -jnp.inf)
    l_sc[...] = jnp.zeros_like(l_sc)
    acc_sc[...] = jnp.zeros_like(acc_sc)

    def scores(kb):
        start = pl.multiple_of(kb * tq, tq)
        return lax.dot_general(kb_sc[pl.ds(start, tq), :], q2_sc[...], NT_DIMS, preferred_element_type=F32)

    def weighted_values(kb):
        acc_sc[...] = alpha_sc[...] * acc_sc[...] + jnp.dot(vt_sc[kb], p_sc[...], preferred_element_type=F32)

    def softmax_step(s_raw, bias_ref, beta):
        s = s_raw - bias_ref[...]
        m_prev = m_sc[...]
        m_new = jnp.maximum(m_prev, jnp.max(s, axis=0, keepdims=True) - beta)
        alpha = jnp.exp(m_prev - m_new)
        p = jnp.exp(s - (m_new + beta))
        l_sc[...] = alpha * l_sc[...] + jnp.sum(p, axis=0, keepdims=True)
        m_sc[...] = m_new
        alpha_sc[...] = alpha
        p_sc[...] = p.astype(BF16)

    s_sc[...] = scores(0)
    p_sc[...] = jnp.zeros_like(p_sc)
    alpha_sc[...] = jnp.ones_like(alpha_sc)

    def body(kb, carry):
        s_raw = s_sc[...]
        s_sc[...] = scores(kb + 1)
        weighted_values(jnp.maximum(kb - 1, 0))
        softmax_step(s_raw, bias_sc, slope * ((qi - kb) * tq).astype(F32))
        return carry

    lax.fori_loop(0, qi, body, 0)
    weighted_values(jnp.maximum(qi - 1, 0))
    softmax_step(s_sc[...], bdiag_sc, 0.0)
    weighted_values(qi)

    lam = _lambda(lq1[...], lk1[...], lq2[...], lk2[...])
    o = acc_sc[...] / l_sc[...]
    od = o[:, 0:tq] - lam * o[:, tq:2 * tq]
    od = od * lax.rsqrt(jnp.mean(od * od, axis=0, keepdims=True) + EPS) * gcol_ref[...] * (1.0 - LAM_INIT)
    o_ref[...] = od.T.astype(o_ref.dtype)


def _attn_prompt(q, kt, vf, slopes, lams, subln_g, batch, seq):
    tq = ATT_TQ
    nq = seq // tq
    vec = lambda b, h, i: (0, 0)
    return pl.pallas_call(
        _attn_prompt_kernel,
        grid=(batch, N_HEADS, nq),
        in_specs=[
            pl.BlockSpec(memory_space=pltpu.SMEM),
            pl.BlockSpec((tq, V_DIM), lambda b, h, i: (b * nq + i, h)),
            pl.BlockSpec((V_DIM, seq), lambda b, h, i: (b * N_HEADS + h, 0)),
            pl.BlockSpec((seq * N_HEADS, V_DIM), lambda b, h, i: (b, 0)),
            pl.BlockSpec((1, HEAD_DIM), vec),
            pl.BlockSpec((1, HEAD_DIM), vec),
            pl.BlockSpec((1, HEAD_DIM), vec),
            pl.BlockSpec((1, HEAD_DIM), vec),
            pl.BlockSpec((V_DIM, 1), vec),
        ],
        out_specs=pl.BlockSpec((tq, V_DIM), lambda b, h, i: (b * nq + i, h)),
        out_shape=jax.ShapeDtypeStruct((batch * seq, N_HEADS * V_DIM), BF16),
        scratch_shapes=[
            pltpu.VMEM((seq, V_DIM), BF16),
            pltpu.VMEM((nq, V_DIM, tq), BF16),
            pltpu.VMEM((2 * tq, V_DIM), BF16),
            pltpu.VMEM((tq, 2 * tq), F32),
            pltpu.VMEM((tq, 2 * tq), F32),
            pltpu.VMEM((tq, 2 * tq), F32),
            pltpu.VMEM((tq, 2 * tq), BF16),
            pltpu.VMEM((1, 2 * tq), F32),
            pltpu.VMEM((1, 2 * tq), F32),
            pltpu.VMEM((1, 2 * tq), F32),
            pltpu.VMEM((V_DIM, 2 * tq), F32),
        ],
        compiler_params=_params(("arbitrary", "arbitrary", "arbitrary")),
        name="attn_prompt",
    )(slopes, q, kt, vf, *lams, subln_g.reshape(V_DIM, 1))


def _attn_sample_kernel(past_len, pt_ref, q_ref, kn_ref, vn_ref, *rest):
    npg = PAGES_PER_STEP
    k_refs = rest[0:npg]
    v_refs = rest[npg:2 * npg]
    lq1, lk1, lq2, lk2, g_ref, o_ref, kt_sc, vb_sc, qbd_sc, sb_sc, m_sc, l_sc, acc_sc = rest[2 * npg:]
    j = pl.program_id(1)
    dec = q_ref.shape[1]
    width = q_ref.shape[2]
    rows = 2 * N_HEADS * dec
    rows_per_head = 2 * dec
    keys = npg * PAGE_SIZE
    chunk = 256
    rows_per_chunk = (chunk // HEAD_DIM) * dec

    row1 = lax.broadcasted_iota(jnp.int32, (rows, 1), 0)
    slope = jnp.zeros((rows, 1), F32)
    for h in range(N_HEADS):
        slope = jnp.where(row1 // rows_per_head == h, 2.0 ** -(h + 1), slope)

    @pl.when(j == 0)
    def _():
        m_sc[...] = jnp.full_like(m_sc, -jnp.inf)
        l_sc[...] = jnp.zeros_like(l_sc)
        acc_sc[...] = jnp.zeros_like(acc_sc)
        qt = jnp.concatenate([q_ref[0]] * (rows // dec), axis=0)
        rr = lax.broadcasted_iota(jnp.int32, (rows, width), 0)
        cc = lax.broadcasted_iota(jnp.int32, (rows, width), 1)
        qbd_sc[...] = jnp.where(cc // HEAD_DIM == rr // dec, qt, 0.0).astype(BF16)
        qpos = lax.broadcasted_iota(jnp.int32, (rows, keys), 0) % dec
        col = lax.broadcasted_iota(jnp.int32, (rows, keys), 1)
        sb_sc[...] = slope * (past_len + qpos - col).astype(F32)

    for r in range(npg):
        kt_sc[:, r * PAGE_SIZE:(r + 1) * PAGE_SIZE] = k_refs[r][0].astype(BF16)
        for h in range(N_HEADS):
            vb_sc[h, r * PAGE_SIZE:(r + 1) * PAGE_SIZE, :] = v_refs[r][
                0, pl.ds(h, PAGE_SIZE, stride=N_HEADS), :].astype(BF16)

    gamma = slope * (j * keys).astype(F32)

    def update(r0, nrows, s, g, pv_fn):
        sl = slice(r0, r0 + nrows)
        m_prev = m_sc[sl]
        m_new = jnp.maximum(m_prev, jnp.max(s, axis=-1, keepdims=True) + g)
        alpha = jnp.exp(m_prev - m_new)
        p = jnp.exp(s - (m_new - g))
        l_sc[sl] = alpha * l_sc[sl] + jnp.sum(p, axis=-1, keepdims=True)
        acc_sc[sl] = alpha * acc_sc[sl] + pv_fn(p.astype(BF16))
        m_sc[sl] = m_new

    n_chunks = width // chunk
    heads_per_chunk = rows_per_chunk // rows_per_head
    scores = [jnp.dot(qbd_sc[c * rows_per_chunk:(c + 1) * rows_per_chunk, c * chunk:(c + 1) * chunk],
                      kt_sc[c * chunk:(c + 1) * chunk, :], preferred_element_type=F32) for c in range(n_chunks)]
    probs, alphas = [], []
    for c in range(n_chunks):
        sl = slice(c * rows_per_chunk, (c + 1) * rows_per_chunk)
        s = scores[c] - sb_sc[sl, :]
        g = gamma[sl]
        m_prev = m_sc[sl]
        m_new = jnp.maximum(m_prev, jnp.max(s, axis=-1, keepdims=True) + g)
        alpha = jnp.exp(m_prev - m_new)
        p = jnp.exp(s - (m_new - g))
        l_sc[sl] = alpha * l_sc[sl] + jnp.sum(p, axis=-1, keepdims=True)
        m_sc[sl] = m_new
        probs.append(p.astype(BF16))
        alphas.append(alpha)
    for c in range(n_chunks):
        sl = slice(c * rows_per_chunk, (c + 1) * rows_per_chunk)
        pv = jnp.concatenate(
            [jnp.dot(probs[c][i * rows_per_head:(i + 1) * rows_per_head], vb_sc[c * heads_per_chunk + i],
                     preferred_element_type=F32) for i in range(heads_per_chunk)], axis=0)
        acc_sc[sl] = alphas[c] * acc_sc[sl] + pv

    @pl.when(j == pl.num_programs(1) - 1)
    def _():
        fill = jnp.zeros((PAGE_SIZE - dec, width), F32)
        k_new = jnp.concatenate([kn_ref[0], fill], axis=0).astype(BF16)
        v_new = jnp.concatenate([vn_ref[0], fill], axis=0).astype(BF16)
        qpos = lax.broadcasted_iota(jnp.int32, (rows, PAGE_SIZE), 0) % dec
        col = lax.broadcasted_iota(jnp.int32, (rows, PAGE_SIZE), 1)
        dist = (qpos - col).astype(F32)
        s = lax.dot_general(qbd_sc[...], k_new, NT_DIMS, preferred_element_type=F32)
        s = jnp.where(dist >= 0.0, s - slope * dist, NEG_INF)

        def pv(pb):
            return jnp.concatenate(
                [jnp.dot(pb[h * rows_per_head:(h + 1) * rows_per_head], v_new[:, h * V_DIM:(h + 1) * V_DIM],
                         preferred_element_type=F32) for h in range(N_HEADS)], axis=0)

        update(0, rows, s, 0.0, pv)

        lam = _lambda(lq1[...], lk1[...], lq2[...], lk2[...])
        o = acc_sc[...] / l_sc[...]
        for h in range(N_HEADS):
            r0 = h * rows_per_head
            od = o[r0:r0 + dec] - lam * o[r0 + dec:r0 + 2 * dec]
            o_ref[0, :, h * V_DIM:(h + 1) * V_DIM] = (_rms(od, g_ref[...]) * (1.0 - LAM_INIT)).astype(o_ref.dtype)


def _attn_sample(q3, k3, v3, col0, cache_kt, cache_vf, page_table, lams, subln_g, past_len):
    nb, dec, _ = q3.shape
    width = N_HEADS * V_DIM
    npg = PAGES_PER_STEP
    n_pages = page_table.shape[1]
    rows = 2 * N_HEADS * dec
    keys = npg * PAGE_SIZE
    tok = pl.BlockSpec((1, dec, width), lambda b, j, pt: (b, 0, 0))
    toks = [pl.BlockSpec((1, dec, width), functools.partial(lambda c, b, j, pt: (b, 0, c), c)) for c in col0]
    vec = lambda b, j, pt: (0, 0)

    def page_spec(r, shape):
        return pl.BlockSpec((1,) + shape, lambda b, j, pt: (pt[b * n_pages + j * npg + r], 0, 0))

    grid_spec = pltpu.PrefetchScalarGridSpec(
        num_scalar_prefetch=1,
        grid=(nb, n_pages // npg),
        in_specs=toks
        + [page_spec(r, cache_kt.shape[1:]) for r in range(npg)]
        + [page_spec(r, cache_vf.shape[1:]) for r in range(npg)]
        + [pl.BlockSpec((1, HEAD_DIM), vec)] * 4
        + [pl.BlockSpec((1, V_DIM), vec)],
        out_specs=tok,
        scratch_shapes=[
            pltpu.VMEM((width, keys), BF16),
            pltpu.VMEM((N_HEADS, keys, V_DIM), BF16),
            pltpu.VMEM((rows, width), BF16),
            pltpu.VMEM((rows, keys), F32),
            pltpu.VMEM((rows, 1), F32),
            pltpu.VMEM((rows, 1), F32),
            pltpu.VMEM((rows, V_DIM), F32),
        ],
    )
    return pl.pallas_call(
        functools.partial(_attn_sample_kernel, past_len),
        grid_spec=grid_spec,
        out_shape=jax.ShapeDtypeStruct((nb, dec, width), F32),
        compiler_params=_params(("arbitrary", "arbitrary")),
        name="attn_sample",
    )(page_table.reshape(-1), q3, k3, v3, *([cache_kt] * npg), *([cache_vf] * npg), *lams, subln_g)


def _merge_kernel(c_ref, od_ref, ga_ref, gb_ref, x_ref, wco_ref, wao_ref, wo_ref, g_ref, x1_ref, h2_ref):
    out_a = jnp.dot(c_ref[...].astype(BF16), wco_ref[...], preferred_element_type=F32)
    out_b = jnp.dot(od_ref[...].astype(BF16), wao_ref[...], preferred_element_type=F32)
    merged = ga_ref[...] * out_a + gb_ref[...] * out_b
    x1 = x_ref[...] + jnp.dot(merged.astype(BF16), wo_ref[...], preferred_element_type=F32)
    x1_ref[...] = x1
    h2_ref[...] = _rms(x1, g_ref[...]).astype(BF16)


def _merge(c_act, od, gates, x, wco, wao, wo, g, tm):
    t = x.shape[0]
    row = lambda i: (i, 0)
    fixed = lambda i: (0, 0)
    resident = functools.partial(pl.BlockSpec, index_map=fixed, pipeline_mode=pl.Buffered(1))
    return pl.pallas_call(
        _merge_kernel,
        grid=(t // tm,),
        in_specs=[
            pl.BlockSpec((tm, D_CONV), row),
            pl.BlockSpec((tm, N_HEADS * V_DIM), row),
            pl.BlockSpec((tm, D_MODEL), row),
            pl.BlockSpec((tm, D_MODEL), lambda i: (i, 1)),
            pl.BlockSpec((tm, D_MODEL), row),
            resident((D_CONV, D_MODEL)),
            resident((N_HEADS * V_DIM, D_MODEL)),
            resident((D_MODEL, D_MODEL)),
            pl.BlockSpec((1, D_MODEL), fixed),
        ],
        out_specs=[pl.BlockSpec((tm, D_MODEL), row), pl.BlockSpec((tm, D_MODEL), row)],
        out_shape=[jax.ShapeDtypeStruct((t, D_MODEL), F32), jax.ShapeDtypeStruct((t, D_MODEL), BF16)],
        compiler_params=_params(("arbitrary",)),
        name="merge",
    )(c_act, od, gates, gates, x, wco, wao, wo, g)


def _ffn_act(gc, h2, wv_ref):
    val = jnp.dot(h2, wv_ref[...], preferred_element_type=F32)
    return (gc * _sigmoid(gc) * val).astype(BF16)


def _ffn_up_prompt_kernel(tiles_per_seq, h2_ref, halo_ref, wg_ref, wv_ref, cw_ref, cb_ref, act_ref, tail_ref,
                          gext_sc):
    i = pl.program_id(0)
    tm = h2_ref.shape[0]
    h2 = h2_ref[...]
    gate = jnp.dot(h2, wg_ref[...], preferred_element_type=F32)
    gate_halo = jnp.dot(halo_ref[...], wg_ref[...], preferred_element_type=F32)
    first = (i % tiles_per_seq) == 0
    gext_sc[0:FFN_HALO, :] = jnp.where(first, 0.0, gate_halo)
    gext_sc[FFN_HALO:, :] = gate
    tail_ref[0] = gate[tm - 8:tm, :]
    gc = (cw_ref[0:1, :] * gext_sc[pl.ds(FFN_HALO - 2, tm), :]
          + cw_ref[1:2, :] * gext_sc[pl.ds(FFN_HALO - 1, tm), :]
          + cw_ref[2:3, :] * gate + cb_ref[...])
    act_ref[...] = _ffn_act(gc, h2, wv_ref)


def _ffn_up_sample_kernel(dec, h2_ref, e0_ref, e1_ref, wg_ref, wv_ref, cw_ref, cb_ref, act_ref, gate_ref):
    h2 = h2_ref[...]
    gate = jnp.dot(h2, wg_ref[...], preferred_element_type=F32)
    gate_ref[...] = gate
    pos = lax.broadcasted_iota(jnp.int32, gate.shape, 0) % dec
    g1 = jnp.where(pos == 0, e1_ref[...], pltpu.roll(gate, 1, 0))
    g2 = jnp.where(pos <= 1, e0_ref[...], pltpu.roll(gate, 2, 0))
    gc = cw_ref[0:1, :] * g2 + cw_ref[1:2, :] * g1 + cw_ref[2:3, :] * gate + cb_ref[...]
    act_ref[...] = _ffn_act(gc, h2, wv_ref)


def _ffn_up_specs(tm, tf):
    nf = D_FF // tf
    return dict(
        h2=pl.BlockSpec((tm, D_MODEL), lambda i, f: (i, 0)),
        wg=pl.BlockSpec((D_MODEL, tf), lambda i, f: (0, f)),
        wv=pl.BlockSpec((D_MODEL, tf), lambda i, f: (0, nf + f)),
        cw=pl.BlockSpec((FFN_CONV_WIDTH, tf), lambda i, f: (0, f)),
        cb=pl.BlockSpec((1, tf), lambda i, f: (0, f)),
        act=pl.BlockSpec((tm, tf), lambda i, f: (i, f)),
    )


def _ffn_up_prompt(h2, w_in_bf, cw, cb, seq, tm):
    t = h2.shape[0]
    tf = FFN_TF
    sp = _ffn_up_specs(tm, tf)
    halo_per_tile = tm // FFN_HALO
    return pl.pallas_call(
        functools.partial(_ffn_up_prompt_kernel, seq // tm),
        grid=(t // tm, D_FF // tf),
        in_specs=[
            sp["h2"],
            pl.BlockSpec((FFN_HALO, D_MODEL), lambda i, f: (jnp.maximum(i * halo_per_tile - 1, 0), 0)),
            sp["wg"], sp["wv"], sp["cw"], sp["cb"],
        ],
        out_specs=[sp["act"], pl.BlockSpec((1, 8, tf), lambda i, f: (i, 0, f))],
        out_shape=[jax.ShapeDtypeStruct((t, D_FF), BF16), jax.ShapeDtypeStruct((t // tm, 8, D_FF), F32)],
        scratch_shapes=[pltpu.VMEM((tm + FFN_HALO, tf), F32)],
        compiler_params=_params(("arbitrary", "arbitrary")),
        name="ffn_up_prompt",
    )(h2, h2, w_in_bf, w_in_bf, cw, cb)


def _ffn_up_sample(h2, e0, e1, w_in_bf, cw, cb, dec):
    t = h2.shape[0]
    tf = FFN_TF
    sp = _ffn_up_specs(t, tf)
    hist = pl.BlockSpec((t, tf), lambda i, f: (0, f))
    return pl.pallas_call(
        functools.partial(_ffn_up_sample_kernel, dec),
        grid=(1, D_FF // tf),
        in_specs=[sp["h2"], hist, hist, sp["wg"], sp["wv"], sp["cw"], sp["cb"]],
        out_specs=[sp["act"], hist],
        out_shape=[jax.ShapeDtypeStruct((t, D_FF), BF16), jax.ShapeDtypeStruct((t, D_FF), F32)],
        compiler_params=_params(("arbitrary", "arbitrary")),
        name="ffn_up_sample",
    )(h2, e0, e1, w_in_bf, w_in_bf, cw, cb)


def _ffn_down_kernel(act_ref, w_ref, x1_ref, g_ref, y_ref):
    y_ref[...] = _rms(x1_ref[...] + jnp.dot(act_ref[...], w_ref[...], preferred_element_type=F32), g_ref[...])


def _ffn_down(act, x1, w_out_bf, gfin, tm):
    t = act.shape[0]
    row = lambda i: (i, 0)
    fixed = lambda i: (0, 0)
    return pl.pallas_call(
        _ffn_down_kernel,
        grid=(t // tm,),
        in_specs=[
            pl.BlockSpec((tm, D_FF), row),
            pl.BlockSpec((D_FF, D_MODEL), fixed, pipeline_mode=pl.Buffered(1)),
            pl.BlockSpec((tm, D_MODEL), row),
            pl.BlockSpec((1, D_MODEL), fixed),
        ],
        out_specs=pl.BlockSpec((tm, D_MODEL), row),
        out_shape=jax.ShapeDtypeStruct((t, D_MODEL), F32),
        compiler_params=_params(("arbitrary",)),
        name="ffn_down",
    )(act, w_out_bf, x1, gfin)


def kernel(x_prompt, x_sample, cache_k, cache_v, state_conv, state_ffn, page_table, norm_mix_g, w_in, conv_w,
           conv_b, conv_ln_g, conv_ln_b, w_conv_out, lambda_q1, lambda_k1, lambda_q2, lambda_k2, subln_g,
           w_attn_out, w_o, norm_ffn_g, w_ffn_in, ffn_conv_w, ffn_conv_b, w_ffn_out, norm_final_g):
    batch, seq, _ = x_prompt.shape
    nb, dec, _ = x_sample.shape
    n_pool = cache_k.shape[1]
    past_len = page_table.shape[1] * PAGE_SIZE
    width = N_HEADS * V_DIM
    layer = 0

    row = lambda a: a[layer].reshape(1, -1)
    w_in_bf = w_in[layer].astype(BF16)
    wco_bf = w_conv_out[layer].astype(BF16)
    wao_bf = w_attn_out[layer].astype(BF16)
    wo_bf = w_o[layer].astype(BF16)
    g_mix, g_ffn, g_fin = row(norm_mix_g), row(norm_ffn_g), norm_final_g.reshape(1, -1)
    cw, cb, lng, lnb = conv_w[layer], row(conv_b), row(conv_ln_g), row(conv_ln_b)
    fcw, fcb = ffn_conv_w[layer], row(ffn_conv_b)
    lams = (row(lambda_q1), row(lambda_k1), row(lambda_q2), row(lambda_k2))
    sub_g = row(subln_g)
    slopes = jnp.asarray([2.0 ** -(h + 1) for h in range(N_HEADS)], F32)

    xp = x_prompt.reshape(batch * seq, D_MODEL)
    u_p, (q_p, kt_p, vf_p), gates_p, (wfo_bf, wfi_bf) = _in_proj(
        xp, g_mix, w_in_bf, 512, 1024, prompt_seq=seq, casts=(w_ffn_out[layer], w_ffn_in[layer]))
    c_p = _conv_prompt(u_p, cw, cb, lng, lnb, seq, 512)
    od_p = _attn_prompt(q_p, kt_p, vf_p, slopes, lams, sub_g, batch, seq)
    x1_p, h2_p = _merge(c_p, od_p, gates_p, xp, wco_bf, wao_bf, wo_bf, g_ffn, 256)
    ffn_tm = 1024
    act_p, tail_p = _ffn_up_prompt(h2_p, wfi_bf, fcw, fcb, seq, ffn_tm)
    y_p = _ffn_down(act_p, x1_p, wfo_bf, g_fin, 256)

    xs = x_sample.reshape(nb * dec, D_MODEL)
    u_s, qkv_s, gates_s, _ = _in_proj(xs, g_mix, w_in_bf, nb * dec, nb * dec)
    c_s, conv_s = _conv_sample(u_s.reshape(nb, dec, D_CONV), state_conv[layer], cw, cb, lng, lnb)
    cache_kt = jnp.transpose(cache_k[layer], (0, 2, 3, 4, 1)).reshape(n_pool, width, PAGE_SIZE)
    cache_vf = cache_v[layer].reshape(n_pool, PAGE_SIZE * N_HEADS, V_DIM)
    qkv_s3 = qkv_s.reshape(nb, dec, 3 * width)
    od_s = _attn_sample(qkv_s3, qkv_s3, qkv_s3, (0, 1, 2), cache_kt, cache_vf, page_table, lams, sub_g, past_len)
    x1_s, h2_s = _merge(c_s.reshape(nb * dec, D_CONV), od_s.reshape(nb * dec, width), gates_s, xs,
                        wco_bf, wao_bf, wo_bf, g_ffn, nb * dec)
    st = state_ffn[layer]
    pad = lambda a: jnp.pad(a, ((0, 0), (0, dec - a.shape[1]), (0, 0))).reshape(nb * dec, D_FF)
    e0 = pad(st)
    e1 = pad(st[:, 1:2])
    act_s, gate_s = _ffn_up_sample(h2_s, e0, e1, wfi_bf, fcw, fcb, dec)
    y_s = _ffn_down(act_s, x1_s, wfo_bf, g_fin, nb * dec)

    tiles_per_seq = seq // ffn_tm
    ffn_p = tail_p.reshape(batch, tiles_per_seq, 8, D_FF)[:, -1, 8 - (FFN_CONV_WIDTH - 1):, :]
    return (
        y_p.reshape(batch, seq, D_MODEL),
        y_s.reshape(nb, dec, D_MODEL),
        jnp.transpose(kt_p.reshape(1, batch, N_HEADS, 2, HEAD_DIM, seq), (0, 1, 5, 2, 3, 4)),
        vf_p.reshape(1, batch, seq, N_HEADS, V_DIM),
        u_p.reshape(batch, seq, D_CONV)[None, :, seq - (CONV_WIDTH - 1):, :],
        ffn_p[None],
        qkv_s[:, width:2 * width].reshape(1, nb, dec, N_HEADS, 2, HEAD_DIM),
        qkv_s[:, 2 * width:].reshape(1, nb, dec, N_HEADS, V_DIM),
        conv_s[None],
        gate_s.reshape(nb, dec, D_FF)[None, :, dec - (FFN_CONV_WIDTH - 1):, :],
    )
```

```python
import functools
import math

import jax
import jax.numpy as jnp
from jax import lax
from jax.experimental import pallas as pl
from jax.experimental.pallas import tpu as pltpu

D_MODEL = 2048
N_HEADS = 8
HEAD_DIM = 64
V_DIM = 2 * HEAD_DIM
D_CONV = D_MODEL // 2
CONV_WIDTH = 31
D_FF = 5632
FFN_CONV_WIDTH = 3
EPS = 1e-5
NEG_INF = -1e30
PAGE_SIZE = 128
LAM_INIT = 0.8 - 0.6 * math.exp(-0.3 * 0)

F32 = jnp.float32
BF16 = jnp.bfloat16

VMEM_LIMIT_BYTES = 56 * 1024 * 1024

PROJ_TN = 1024
ATT_TQ = 256
PAGES_PER_STEP = 16
FFN_TF = 512
FFN_HALO = 16
CONV_HALO = 32


NT_DIMS = (((1,), (1,)), ((), ()))


def _params(sem):
    return pltpu.CompilerParams(dimension_semantics=sem, vmem_limit_bytes=VMEM_LIMIT_BYTES)


def _sigmoid(x):
    return 1.0 / (1.0 + jnp.exp(-x))


def _rms(x, g):
    return x * lax.rsqrt(jnp.mean(x * x, axis=-1, keepdims=True) + EPS) * g


def _rms_cast_kernel(x_ref, g_ref, h_ref):
    h_ref[...] = _rms(x_ref[...], g_ref[...]).astype(BF16)


def _rms_cast(x, g, tm):
    t = x.shape[0]
    return pl.pallas_call(
        _rms_cast_kernel,
        grid=(t // tm,),
        in_specs=[pl.BlockSpec((tm, D_MODEL), lambda i: (i, 0)), pl.BlockSpec((1, D_MODEL), lambda i: (0, 0))],
        out_specs=pl.BlockSpec((tm, D_MODEL), lambda i: (i, 0)),
        out_shape=jax.ShapeDtypeStruct((t, D_MODEL), BF16),
        compiler_params=_params(("arbitrary",)),
        name="rms_cast",
    )(x, g)


def _glu_kernel(h_ref, wv_ref, wg_ref, u_ref):
    h = h_ref[...]
    u_ref[...] = jnp.dot(h, wv_ref[...], preferred_element_type=F32) * _sigmoid(
        jnp.dot(h, wg_ref[...], preferred_element_type=F32))


def _qkv_kernel(h_ref, w_ref, o_ref):
    scale = jnp.where(pl.program_id(1) == 0, HEAD_DIM ** -0.5, 1.0)
    o_ref[...] = jnp.dot(h_ref[...], w_ref[...], preferred_element_type=F32) * scale


def _gates_kernel(h_ref, w_ref, o_ref):
    o_ref[...] = _sigmoid(jnp.dot(h_ref[...], w_ref[...], preferred_element_type=F32))


def _with_cast(kern, n_w):
    def wrapped(h_ref, *refs):
        kern(h_ref, *refs[:n_w], refs[n_w + 1])
        refs[n_w + 2][...] = refs[n_w][...].astype(BF16)
    return wrapped


def _proj(kern, name, h, w_bf, tm, tn, n_col, col_tile0, cast=None):
    t = h.shape[0]
    w_specs = [pl.BlockSpec((D_MODEL, tn), functools.partial(lambda c0, i, j: (0, c0 + j), c0))
               for c0 in col_tile0]
    in_specs = [pl.BlockSpec((tm, D_MODEL), lambda i, j: (i, 0))] + w_specs
    out_specs = [pl.BlockSpec((tm, tn), lambda i, j: (i, j))]
    out_shape = [jax.ShapeDtypeStruct((t, tn * n_col), F32)]
    args = [h] + [w_bf] * len(col_tile0)
    if cast is not None:
        slab = cast.shape[0] // ((t // tm) * n_col)
        assert slab * (t // tm) * n_col == cast.shape[0] and slab % 16 == 0, cast.shape
        slab_spec = pl.BlockSpec((slab, cast.shape[1]), lambda i, j: (i * n_col + j, 0))
        in_specs.append(slab_spec)
        out_specs.append(slab_spec)
        out_shape.append(jax.ShapeDtypeStruct(cast.shape, BF16))
        args.append(cast)
        kern = _with_cast(kern, len(col_tile0))
    out = pl.pallas_call(
        kern,
        grid=(t // tm, n_col),
        in_specs=in_specs,
        out_specs=out_specs,
        out_shape=out_shape,
        compiler_params=_params(("arbitrary", "arbitrary")),
        name=name,
    )(*args)
    return out[0] if cast is None else out


def _qkv_prompt_kernel(n_cast, h_ref, wq_ref, wkt_ref, wv_ref, *refs):
    q_ref, kt_ref, vf_ref = refs[n_cast:n_cast + 3]
    for src_ref, dst_ref in zip(refs[:n_cast], refs[n_cast + 3:]):
        dst_ref[...] = src_ref[...].astype(BF16)
    h = h_ref[...]
    tm = h.shape[0]
    q_ref[...] = jnp.dot(h, wq_ref[...], preferred_element_type=F32) * (HEAD_DIM ** -0.5)
    kt_ref[...] = lax.dot_general(wkt_ref[...], h, NT_DIMS, preferred_element_type=F32)
    v = jnp.dot(h, wv_ref[...], preferred_element_type=F32)
    for hd in range(N_HEADS):
        vf_ref[pl.ds(hd, tm, stride=N_HEADS), :] = v[:, hd * V_DIM:(hd + 1) * V_DIM]


def _qkv_prompt(h, w_bf, seq, tm, casts=()):
    t = h.shape[0]
    steps = t // tm
    width = N_HEADS * V_DIM
    tiles_per_seq = seq // tm
    q0 = 2 * D_CONV // width
    wkt = w_bf[:, (q0 + 1) * width:(q0 + 2) * width].T
    fixed = lambda c: pl.BlockSpec((D_MODEL, width), functools.partial(lambda c, i: (0, c), c),
                                   pipeline_mode=pl.Buffered(1))
    for c in casts:
        assert c.shape[0] % (16 * steps) == 0, c.shape
    slab_specs = [pl.BlockSpec((c.shape[0] // steps, c.shape[1]), lambda i: (i, 0)) for c in casts]
    return pl.pallas_call(
        functools.partial(_qkv_prompt_kernel, len(casts)),
        grid=(steps,),
        in_specs=[
            pl.BlockSpec((tm, D_MODEL), lambda i: (i, 0)),
            fixed(q0),
            pl.BlockSpec((width, D_MODEL), lambda i: (0, 0), pipeline_mode=pl.Buffered(1)),
            fixed(q0 + 2),
        ] + slab_specs,
        out_specs=[
            pl.BlockSpec((tm, width), lambda i: (i, 0)),
            pl.BlockSpec((width, tm), lambda i: (i // tiles_per_seq, i % tiles_per_seq)),
            pl.BlockSpec((tm * N_HEADS, V_DIM), lambda i: (i, 0)),
        ] + slab_specs,
        out_shape=[
            jax.ShapeDtypeStruct((t, width), F32),
            jax.ShapeDtypeStruct((t // seq * width, seq), F32),
            jax.ShapeDtypeStruct((t * N_HEADS, V_DIM), F32),
        ] + [jax.ShapeDtypeStruct(c.shape, BF16) for c in casts],
        compiler_params=_params(("arbitrary",)),
        name="proj_qkv_prompt",
    )(h, w_bf, wkt, w_bf, *casts)


def _in_proj(x, g, w_bf, tm, tm_wide, prompt_seq=None, casts=(None, None), qkv_casts=()):
    h = _rms_cast(x, g, tm)
    glu_tn = 512
    u = _proj(_glu_kernel, "proj_glu", h, w_bf, tm_wide, glu_tn, D_CONV // glu_tn, [0, D_CONV // glu_tn],
              cast=casts[0])
    tn = PROJ_TN
    qkv0 = 2 * D_CONV // tn
    if prompt_seq is None:
        qkv = _proj(_qkv_kernel, "proj_qkv", h, w_bf, tm, tn, 3, [qkv0])
    else:
        qkv = _qkv_prompt(h, w_bf, prompt_seq, tm, casts=qkv_casts)
        qkv, qkv_casts = tuple(qkv[:3]), tuple(qkv[3:])
    gates = _proj(_gates_kernel, "proj_gates", h, w_bf, tm_wide, tn, 2 * D_MODEL // tn, [qkv0 + 3],
                  cast=casts[1])
    cast_out = (None if casts[0] is None else u[1], None if casts[1] is None else gates[1])
    if casts[0] is not None:
        u = u[0]
    if casts[1] is not None:
        gates = gates[0]
    return u, qkv, gates, cast_out + tuple(qkv_casts)


def _ln_swish(c, g, b):
    mu = jnp.mean(c, axis=-1, keepdims=True)
    cc = c - mu
    y = cc * lax.rsqrt(jnp.mean(cc * cc, axis=-1, keepdims=True) + EPS) * g + b
    return y * _sigmoid(y)


def _conv_prompt_kernel(tiles_per_seq, rows_chunk, u_ref, halo_ref, w_ref, b_ref, g_ref, beta_ref,
                        o_ref, ext_sc, c_sc, wb_sc):
    i = pl.program_id(0)
    tm = u_ref.shape[0]

    @pl.when(i == 0)
    def _():
        for tap in range(CONV_WIDTH):
            wb_sc[tap] = jnp.broadcast_to(w_ref[tap:tap + 1, :], (8, D_CONV))
        wb_sc[CONV_WIDTH] = jnp.broadcast_to(b_ref[...], (8, D_CONV))

    first = (i % tiles_per_seq) == 0
    ext_sc[0:CONV_HALO, :] = jnp.where(first, 0.0, halo_ref[...])
    ext_sc[CONV_HALO:, :] = u_ref[...]
    lead = CONV_HALO - (CONV_WIDTH - 1)
    col_chunk = 256
    groups = rows_chunk // 8

    def body(r, carry):
        r0 = pl.multiple_of(r * rows_chunk, rows_chunk)
        for c0 in range(0, D_CONV, col_chunk):
            cols = slice(c0, c0 + col_chunk)
            accs = [wb_sc[CONV_WIDTH, :, cols]] * groups
            win = ext_sc[pl.ds(r0, rows_chunk + CONV_HALO), cols]
            for rem in range(8):
                shifted = win if rem == 0 else win[rem:rem + rows_chunk + CONV_HALO - 8]
                for tap in range(CONV_WIDTH):
                    if (lead + tap) % 8 != rem:
                        continue
                    a0 = (lead + tap) - rem
                    w_tap = wb_sc[tap, :, cols]
                    accs = [accs[gi] + w_tap * shifted[a0 + 8 * gi:a0 + 8 * gi + 8] for gi in range(groups)]
            for gi in range(groups):
                c_sc[pl.ds(r0 + 8 * gi, 8), cols] = accs[gi]
        return carry

    lax.fori_loop(0, tm // rows_chunk, body, 0)
    o_ref[...] = _ln_swish(c_sc[...], g_ref[...], beta_ref[...]).astype(o_ref.dtype)


def _conv_prompt(u, w, b, g, beta, seq, tm):
    t = u.shape[0]
    tiles_per_seq = seq // tm
    halo_per_tile = tm // CONV_HALO
    vec = lambda i: (0, 0)
    return pl.pallas_call(
        functools.partial(_conv_prompt_kernel, tiles_per_seq, 32),
        grid=(t // tm,),
        in_specs=[
            pl.BlockSpec((tm, D_CONV), lambda i: (i, 0)),
            pl.BlockSpec((CONV_HALO, D_CONV), lambda i: (jnp.maximum(i * halo_per_tile - 1, 0), 0)),
            pl.BlockSpec((CONV_WIDTH, D_CONV), vec),
            pl.BlockSpec((1, D_CONV), vec),
            pl.BlockSpec((1, D_CONV), vec),
            pl.BlockSpec((1, D_CONV), vec),
        ],
        out_specs=pl.BlockSpec((tm, D_CONV), lambda i: (i, 0)),
        out_shape=jax.ShapeDtypeStruct((t, D_CONV), BF16),
        scratch_shapes=[pltpu.VMEM((tm + CONV_HALO, D_CONV), F32), pltpu.VMEM((tm, D_CONV), F32),
                        pltpu.VMEM((CONV_WIDTH + 1, 8, D_CONV), F32)],
        compiler_params=_params(("arbitrary",)),
        name="conv_prompt",
    )(u, u, w, b, g, beta)


def _conv_sample_kernel(u_ref, st_ref, w_ref, b_ref, g_ref, beta_ref, o_ref, new_ref, ext_sc):
    nb, dec, _ = u_ref.shape
    hist = CONV_WIDTH - 1
    ext_sc[:, 0:hist, :] = st_ref[...]
    ext_sc[:, hist:hist + dec, :] = u_ref[...]
    acc = jnp.broadcast_to(b_ref[...][None], (nb, dec, D_CONV))
    for tap in range(CONV_WIDTH):
        acc = acc + w_ref[tap:tap + 1, :][None] * ext_sc[:, tap:tap + dec, :]
    o_ref[...] = _ln_swish(acc, g_ref[...][None], beta_ref[...][None]).astype(o_ref.dtype)
    new_ref[...] = ext_sc[:, dec:dec + hist, :]


def _conv_sample(u3, state, w, b, g, beta):
    nb, dec, _ = u3.shape
    hist = CONV_WIDTH - 1
    return pl.pallas_call(
        _conv_sample_kernel,
        out_shape=[
            jax.ShapeDtypeStruct((nb, dec, D_CONV), BF16),
            jax.ShapeDtypeStruct((nb, hist, D_CONV), F32),
        ],
        scratch_shapes=[pltpu.VMEM((nb, hist + dec, D_CONV), F32)],
        compiler_params=pltpu.CompilerParams(vmem_limit_bytes=VMEM_LIMIT_BYTES),
        name="conv_sample",
    )(u3, state, w, b, g, beta)


def _lambda(lq1, lk1, lq2, lk2):
    return (jnp.exp(jnp.sum(lq1 * lk1, axis=-1, keepdims=True))
            - jnp.exp(jnp.sum(lq2 * lk2, axis=-1, keepdims=True)) + LAM_INIT)


def _stack_maps(q):
    first = lax.broadcasted_iota(jnp.int32, q.shape, 1) < HEAD_DIM
    zero = jnp.zeros_like(q)
    return jnp.concatenate([jnp.where(first, q, zero), jnp.where(first, zero, q)], axis=0)


def _attn_prompt_kernel(slopes_ref, q_ref, kt_ref, v_ref, lq1, lk1, lq2, lk2, gcol_ref, o_ref,
                        kb_sc, vt_sc, q2_sc, bias_sc, bdiag_sc, s_sc, p_sc, alpha_sc, m_sc, l_sc, acc_sc):
    h = pl.program_id(1)
    qi = pl.program_id(2)
    tq = q_ref.shape[0]
    slope = slopes_ref[h]

    @pl.when(qi == 0)
    def _():
        kb_sc[...] = kt_ref[...].T.astype(BF16)
        for kb in range(kt_ref.shape[1] // tq):
            vt_sc[kb] = v_ref[pl.ds(kb * tq * N_HEADS + h, tq, stride=N_HEADS), :].T.astype(BF16)
        key = lax.broadcasted_iota(jnp.int32, (tq, 2 * tq), 0)
        qry = lax.broadcasted_iota(jnp.int32, (tq, 2 * tq), 1)
        dist = (jnp.where(qry >= tq, qry - tq, qry) - key).astype(F32)
        bias_sc[...] = slope * dist
        bdiag_sc[...] = jnp.where(dist >= 0.0, slope * dist, -NEG_INF)

    q2_sc[...] = _stack_maps(q_ref[...]).astype(BF16)
    m_sc[...] = jnp.full_like(m_sc, -jnp.inf)
    l_sc[...] = jnp.zeros_like(l_sc)
    acc_sc[...] = jnp.zeros_like(acc_sc)

    def scores(kb):
        start = pl.multiple_of(kb * tq, tq)
        return lax.dot_general(kb_sc[pl.ds(start, tq), :], q2_sc[...], NT_DIMS, preferred_element_type=F32)

    def weighted_values(kb):
        acc_sc[...] = alpha_sc[...] * acc_sc[...] + jnp.dot(vt_sc[kb], p_sc[...], preferred_element_type=F32)

    def softmax_step(s_raw, bias_ref, beta):
        s = s_raw - bias_ref[...]
        m_prev = m_sc[...]
        m_new = jnp.maximum(m_prev, jnp.max(s, axis=0, keepdims=True) - beta)
        alpha = jnp.exp(m_prev - m_new)
        p = jnp.exp(s - (m_new + beta))
        l_sc[...] = alpha * l_sc[...] + jnp.sum(p, axis=0, keepdims=True)
        m_sc[...] = m_new
        alpha_sc[...] = alpha
        p_sc[...] = p.astype(BF16)

    s_sc[...] = scores(0)
    p_sc[...] = jnp.zeros_like(p_sc)
    alpha_sc[...] = jnp.ones_like(alpha_sc)

    def body(kb, carry):
        s_raw = s_sc[...]
        s_sc[...] = scores(kb + 1)
        weighted_values(jnp.maximum(kb - 1, 0))
        softmax_step(s_raw, bias_sc, slope * ((qi - kb) * tq).astype(F32))
        return carry

    lax.fori_loop(0, qi, body, 0)
    weighted_values(jnp.maximum(qi - 1, 0))
    softmax_step(s_sc[...], bdiag_sc, 0.0)
    weighted_values(qi)

    lam = _lambda(lq1[...], lk1[...], lq2[...], lk2[...])
    o = acc_sc[...] / l_sc[...]
    od = o[:, 0:tq] - lam * o[:, tq:2 * tq]
    od = od * lax.rsqrt(jnp.mean(od * od, axis=0, keepdims=True) + EPS) * gcol_ref[...] * (1.0 - LAM_INIT)
    o_ref[...] = od.T.astype(o_ref.dtype)


def _attn_prompt(q, kt, vf, slopes, lams, subln_g, batch, seq):
    tq = ATT_TQ
    nq = seq // tq
    vec = lambda b, h, i: (0, 0)
    return pl.pallas_call(
        _attn_prompt_kernel,
        grid=(batch, N_HEADS, nq),
        in_specs=[
            pl.BlockSpec(memory_space=pltpu.SMEM),
            pl.BlockSpec((tq, V_DIM), lambda b, h, i: (b * nq + i, h)),
            pl.BlockSpec((V_DIM, seq), lambda b, h, i: (b * N_HEADS + h, 0)),
            pl.BlockSpec((seq * N_HEADS, V_DIM), lambda b, h, i: (b, 0)),
            pl.BlockSpec((1, HEAD_DIM), vec),
            pl.BlockSpec((1, HEAD_DIM), vec),
            pl.BlockSpec((1, HEAD_DIM), vec),
            pl.BlockSpec((1, HEAD_DIM), vec),
            pl.BlockSpec((V_DIM, 1), vec),
        ],
        out_specs=pl.BlockSpec((tq, V_DIM), lambda b, h, i: (b * nq + i, h)),
        out_shape=jax.ShapeDtypeStruct((batch * seq, N_HEADS * V_DIM), BF16),
        scratch_shapes=[
            pltpu.VMEM((seq, V_DIM), BF16),
            pltpu.VMEM((nq, V_DIM, tq), BF16),
            pltpu.VMEM((2 * tq, V_DIM), BF16),
            pltpu.VMEM((tq, 2 * tq), F32),
            pltpu.VMEM((tq, 2 * tq), F32),
            pltpu.VMEM((tq, 2 * tq), F32),
            pltpu.VMEM((tq, 2 * tq), BF16),
            pltpu.VMEM((1, 2 * tq), F32),
            pltpu.VMEM((1, 2 * tq), F32),
            pltpu.VMEM((1, 2 * tq), F32),
            pltpu.VMEM((V_DIM, 2 * tq), F32),
        ],
        compiler_params=_params(("arbitrary", "arbitrary", "arbitrary")),
        name="attn_prompt",
    )(slopes, q, kt, vf, *lams, subln_g.reshape(V_DIM, 1))


def _attn_sample_kernel(past_len, pt_ref, q_ref, kn_ref, vn_ref, *rest):
    npg = PAGES_PER_STEP
    k_refs = rest[0:npg]
    v_refs = rest[npg:2 * npg]
    lq1, lk1, lq2, lk2, g_ref, o_ref, kt_sc, vb_sc, qbd_sc, sb_sc, m_sc, l_sc, acc_sc = rest[2 * npg:]
    j = pl.program_id(1)
    dec = q_ref.shape[1]
    width = q_ref.shape[2]
    rows = 2 * N_HEADS * dec
    rows_per_head = 2 * dec
    keys = npg * PAGE_SIZE
    chunk = 256
    rows_per_chunk = (chunk // HEAD_DIM) * dec

    row1 = lax.broadcasted_iota(jnp.int32, (rows, 1), 0)
    slope = jnp.zeros((rows, 1), F32)
    for h in range(N_HEADS):
        slope = jnp.where(row1 // rows_per_head == h, 2.0 ** -(h + 1), slope)

    @pl.when(j == 0)
    def _():
        m_sc[...] = jnp.full_like(m_sc, -jnp.inf)
        l_sc[...] = jnp.zeros_like(l_sc)
        acc_sc[...] = jnp.zeros_like(acc_sc)
        qt = jnp.concatenate([q_ref[0]] * (rows // dec), axis=0)
        rr = lax.broadcasted_iota(jnp.int32, (rows, width), 0)
        cc = lax.broadcasted_iota(jnp.int32, (rows, width), 1)
        qbd_sc[...] = jnp.where(cc // HEAD_DIM == rr // dec, qt, 0.0).astype(BF16)
        qpos = lax.broadcasted_iota(jnp.int32, (rows, keys), 0) % dec
        col = lax.broadcasted_iota(jnp.int32, (rows, keys), 1)
        sb_sc[...] = slope * (past_len + qpos - col).astype(F32)

    for r in range(npg):
        kt_sc[:, r * PAGE_SIZE:(r + 1) * PAGE_SIZE] = k_refs[r][0].astype(BF16)
        for h in range(N_HEADS):
            vb_sc[h, r * PAGE_SIZE:(r + 1) * PAGE_SIZE, :] = v_refs[r][
                0, pl.ds(h, PAGE_SIZE, stride=N_HEADS), :].astype(BF16)

    gamma = slope * (j * keys).astype(F32)

    def update(r0, nrows, s, g, pv_fn):
        sl = slice(r0, r0 + nrows)
        m_prev = m_sc[sl]
        m_new = jnp.maximum(m_prev, jnp.max(s, axis=-1, keepdims=True) + g)
        alpha = jnp.exp(m_prev - m_new)
        p = jnp.exp(s - (m_new - g))
        l_sc[sl] = alpha * l_sc[sl] + jnp.sum(p, axis=-1, keepdims=True)
        acc_sc[sl] = alpha * acc_sc[sl] + pv_fn(p.astype(BF16))
        m_sc[sl] = m_new

    n_chunks = width // chunk
    heads_per_chunk = rows_per_chunk // rows_per_head
    scores = [jnp.dot(qbd_sc[c * rows_per_chunk:(c + 1) * rows_per_chunk, c * chunk:(c + 1) * chunk],
                      kt_sc[c * chunk:(c + 1) * chunk, :], preferred_element_type=F32) for c in range(n_chunks)]
    probs, alphas = [], []
    for c in range(n_chunks):
        sl = slice(c * rows_per_chunk, (c + 1) * rows_per_chunk)
        s = scores[c] - sb_sc[sl, :]
        g = gamma[sl]
        m_prev = m_sc[sl]
        m_new = jnp.maximum(m_prev, jnp.max(s, axis=-1, keepdims=True) + g)
        alpha = jnp.exp(m_prev - m_new)
        p = jnp.exp(s - (m_new - g))
        l_sc[sl] = alpha * l_sc[sl] + jnp.sum(p, axis=-1, keepdims=True)
        m_sc[sl] = m_new
        probs.append(p.astype(BF16))
        alphas.append(alpha)
    for c in range(n_chunks):
        sl = slice(c * rows_per_chunk, (c + 1) * rows_per_chunk)
        pv = jnp.concatenate(
            [jnp.dot(probs[c][i * rows_per_head:(i + 1) * rows_per_head], vb_sc[c * heads_per_chunk + i],
                     preferred_element_type=F32) for i in range(heads_per_chunk)], axis=0)
        acc_sc[sl] = alphas[c] * acc_sc[sl] + pv

    @pl.when(j == pl.num_programs(1) - 1)
    def _():
        fill = jnp.zeros((PAGE_SIZE - dec, width), F32)
        k_new = jnp.concatenate([kn_ref[0], fill], axis=0).astype(BF16)
        v_new = jnp.concatenate([vn_ref[0], fill], axis=0).astype(BF16)
        qpos = lax.broadcasted_iota(jnp.int32, (rows, PAGE_SIZE), 0) % dec
        col = lax.broadcasted_iota(jnp.int32, (rows, PAGE_SIZE), 1)
        dist = (qpos - col).astype(F32)
        s = lax.dot_general(qbd_sc[...], k_new, NT_DIMS, preferred_element_type=F32)
        s = jnp.where(dist >= 0.0, s - slope * dist, NEG_INF)

        def pv(pb):
            return jnp.concatenate(
                [jnp.dot(pb[h * rows_per_head:(h + 1) * rows_per_head], v_new[:, h * V_DIM:(h + 1) * V_DIM],
                         preferred_element_type=F32) for h in range(N_HEADS)], axis=0)

        update(0, rows, s, 0.0, pv)

        lam = _lambda(lq1[...], lk1[...], lq2[...], lk2[...])
        o = acc_sc[...] / l_sc[...]
        for h in range(N_HEADS):
            r0 = h * rows_per_head
            od = o[r0:r0 + dec] - lam * o[r0 + dec:r0 + 2 * dec]
            o_ref[0, :, h * V_DIM:(h + 1) * V_DIM] = (_rms(od, g_ref[...]) * (1.0 - LAM_INIT)).astype(o_ref.dtype)


def _attn_sample(q3, k3, v3, col0, cache_kt, cache_vf, page_table, lams, subln_g, past_len):
    nb, dec, _ = q3.shape
    width = N_HEADS * V_DIM
    npg = PAGES_PER_STEP
    n_pages = page_table.shape[1]
    rows = 2 * N_HEADS * dec
    keys = npg * PAGE_SIZE
    tok = pl.BlockSpec((1, dec, width), lambda b, j, pt: (b, 0, 0))
    toks = [pl.BlockSpec((1, dec, width), functools.partial(lambda c, b, j, pt: (b, 0, c), c)) for c in col0]
    vec = lambda b, j, pt: (0, 0)

    def page_spec(r, shape):
        return pl.BlockSpec((1,) + shape, lambda b, j, pt: (pt[b * n_pages + j * npg + r], 0, 0))

    grid_spec = pltpu.PrefetchScalarGridSpec(
        num_scalar_prefetch=1,
        grid=(nb, n_pages // npg),
        in_specs=toks
        + [page_spec(r, cache_kt.shape[1:]) for r in range(npg)]
        + [page_spec(r, cache_vf.shape[1:]) for r in range(npg)]
        + [pl.BlockSpec((1, HEAD_DIM), vec)] * 4
        + [pl.BlockSpec((1, V_DIM), vec)],
        out_specs=tok,
        scratch_shapes=[
            pltpu.VMEM((width, keys), BF16),
            pltpu.VMEM((N_HEADS, keys, V_DIM), BF16),
            pltpu.VMEM((rows, width), BF16),
            pltpu.VMEM((rows, keys), F32),
            pltpu.VMEM((rows, 1), F32),
            pltpu.VMEM((rows, 1), F32),
            pltpu.VMEM((rows, V_DIM), F32),
        ],
    )
    return pl.pallas_call(
        functools.partial(_attn_sample_kernel, past_len),
        grid_spec=grid_spec,
        out_shape=jax.ShapeDtypeStruct((nb, dec, width), F32),
        compiler_params=_params(("arbitrary", "arbitrary")),
        name="attn_sample",
    )(page_table.reshape(-1), q3, k3, v3, *([cache_kt] * npg), *([cache_vf] * npg), *lams, subln_g)


def _merge_kernel(c_ref, od_ref, ga_ref, gb_ref, x_ref, wco_ref, wao_ref, wo_ref, g_ref, x1_ref, h2_ref):
    out_a = jnp.dot(c_ref[...].astype(BF16), wco_ref[...], preferred_element_type=F32)
    out_b = jnp.dot(od_ref[...].astype(BF16), wao_ref[...], preferred_element_type=F32)
    merged = ga_ref[...] * out_a + gb_ref[...] * out_b
    x1 = x_ref[...] + jnp.dot(merged.astype(BF16), wo_ref[...], preferred_element_type=F32)
    x1_ref[...] = x1
    h2_ref[...] = _rms(x1, g_ref[...]).astype(BF16)


def _merge(c_act, od, gates, x, wco, wao, wo, g, tm):
    t = x.shape[0]
    row = lambda i: (i, 0)
    fixed = lambda i: (0, 0)
    resident = functools.partial(pl.BlockSpec, index_map=fixed, pipeline_mode=pl.Buffered(1))
    return pl.pallas_call(
        _merge_kernel,
        grid=(t // tm,),
        in_specs=[
            pl.BlockSpec((tm, D_CONV), row),
            pl.BlockSpec((tm, N_HEADS * V_DIM), row),
            pl.BlockSpec((tm, D_MODEL), row),
            pl.BlockSpec((tm, D_MODEL), lambda i: (i, 1)),
            pl.BlockSpec((tm, D_MODEL), row),
            resident((D_CONV, D_MODEL)),
            resident((N_HEADS * V_DIM, D_MODEL)),
            resident((D_MODEL, D_MODEL)),
            pl.BlockSpec((1, D_MODEL), fixed),
        ],
        out_specs=[pl.BlockSpec((tm, D_MODEL), row), pl.BlockSpec((tm, D_MODEL), row)],
        out_shape=[jax.ShapeDtypeStruct((t, D_MODEL), F32), jax.ShapeDtypeStruct((t, D_MODEL), BF16)],
        compiler_params=_params(("arbitrary",)),
        name="merge",
    )(c_act, od, gates, gates, x, wco, wao, wo, g)


def _ffn_act(gc, h2, wv_ref):
    val = jnp.dot(h2, wv_ref[...], preferred_element_type=F32)
    return (gc * _sigmoid(gc) * val).astype(BF16)


def _ffn_up_prompt_kernel(tiles_per_seq, h2_ref, halo_ref, wg_ref, wv_ref, cw_ref, cb_ref, act_ref, tail_ref,
                          gext_sc):
    i = pl.program_id(0)
    tm = h2_ref.shape[0]
    h2 = h2_ref[...]
    gate = jnp.dot(h2, wg_ref[...], preferred_element_type=F32)
    gate_halo = jnp.dot(halo_ref[...], wg_ref[...], preferred_element_type=F32)
    first = (i % tiles_per_seq) == 0
    gext_sc[0:FFN_HALO, :] = jnp.where(first, 0.0, gate_halo)
    gext_sc[FFN_HALO:, :] = gate
    tail_ref[0] = gate[tm - 8:tm, :]
    gc = (cw_ref[0:1, :] * gext_sc[pl.ds(FFN_HALO - 2, tm), :]
          + cw_ref[1:2, :] * gext_sc[pl.ds(FFN_HALO - 1, tm), :]
          + cw_ref[2:3, :] * gate + cb_ref[...])
    act_ref[...] = _ffn_act(gc, h2, wv_ref)


def _ffn_up_sample_kernel(dec, h2_ref, e0_ref, e1_ref, wg_ref, wv_ref, cw_ref, cb_ref, act_ref, gate_ref):
    h2 = h2_ref[...]
    gate = jnp.dot(h2, wg_ref[...], preferred_element_type=F32)
    gate_ref[...] = gate
    pos = lax.broadcasted_iota(jnp.int32, gate.shape, 0) % dec
    g1 = jnp.where(pos == 0, e1_ref[...], pltpu.roll(gate, 1, 0))
    g2 = jnp.where(pos <= 1, e0_ref[...], pltpu.roll(gate, 2, 0))
    gc = cw_ref[0:1, :] * g2 + cw_ref[1:2, :] * g1 + cw_ref[2:3, :] * gate + cb_ref[...]
    act_ref[...] = _ffn_act(gc, h2, wv_ref)


def _ffn_up_specs(tm, tf):
    nf = D_FF // tf
    return dict(
        h2=pl.BlockSpec((tm, D_MODEL), lambda i, f: (i, 0)),
        wg=pl.BlockSpec((D_MODEL, tf), lambda i, f: (0, f)),
        wv=pl.BlockSpec((D_MODEL, tf), lambda i, f: (0, nf + f)),
        cw=pl.BlockSpec((FFN_CONV_WIDTH, tf), lambda i, f: (0, f)),
        cb=pl.BlockSpec((1, tf), lambda i, f: (0, f)),
        act=pl.BlockSpec((tm, tf), lambda i, f: (i, f)),
    )


def _ffn_up_prompt(h2, w_in_bf, cw, cb, seq, tm):
    t = h2.shape[0]
    tf = FFN_TF
    sp = _ffn_up_specs(tm, tf)
    halo_per_tile = tm // FFN_HALO
    return pl.pallas_call(
        functools.partial(_ffn_up_prompt_kernel, seq // tm),
        grid=(t // tm, D_FF // tf),
        in_specs=[
            sp["h2"],
            pl.BlockSpec((FFN_HALO, D_MODEL), lambda i, f: (jnp.maximum(i * halo_per_tile - 1, 0), 0)),
            sp["wg"], sp["wv"], sp["cw"], sp["cb"],
        ],
        out_specs=[sp["act"], pl.BlockSpec((1, 8, tf), lambda i, f: (i, 0, f))],
        out_shape=[jax.ShapeDtypeStruct((t, D_FF), BF16), jax.ShapeDtypeStruct((t // tm, 8, D_FF), F32)],
        scratch_shapes=[pltpu.VMEM((tm + FFN_HALO, tf), F32)],
        compiler_params=_params(("arbitrary", "arbitrary")),
        name="ffn_up_prompt",
    )(h2, h2, w_in_bf, w_in_bf, cw, cb)


def _ffn_up_sample(h2, e0, e1, w_in_bf, cw, cb, dec):
    t = h2.shape[0]
    tf = FFN_TF
    sp = _ffn_up_specs(t, tf)
    hist = pl.BlockSpec((t, tf), lambda i, f: (0, f))
    return pl.pallas_call(
        functools.partial(_ffn_up_sample_kernel, dec),
        grid=(1, D_FF // tf),
        in_specs=[sp["h2"], hist, hist, sp["wg"], sp["wv"], sp["cw"], sp["cb"]],
        out_specs=[sp["act"], hist],
        out_shape=[jax.ShapeDtypeStruct((t, D_FF), BF16), jax.ShapeDtypeStruct((t, D_FF), F32)],
        compiler_params=_params(("arbitrary", "arbitrary")),
        name="ffn_up_sample",
    )(h2, e0, e1, w_in_bf, w_in_bf, cw, cb)


def _ffn_down_kernel(act_ref, w_ref, x1_ref, g_ref, y_ref):
    y_ref[...] = _rms(x1_ref[...] + jnp.dot(act_ref[...], w_ref[...], preferred_element_type=F32), g_ref[...])


def _ffn_down(act, x1, w_out_bf, gfin, tm):
    t = act.shape[0]
    row = lambda i: (i, 0)
    fixed = lambda i: (0, 0)
    return pl.pallas_call(
        _ffn_down_kernel,
        grid=(t // tm,),
        in_specs=[
            pl.BlockSpec((tm, D_FF), row),
            pl.BlockSpec((D_FF, D_MODEL), fixed, pipeline_mode=pl.Buffered(1)),
            pl.BlockSpec((tm, D_MODEL), row),
            pl.BlockSpec((1, D_MODEL), fixed),
        ],
        out_specs=pl.BlockSpec((tm, D_MODEL), row),
        out_shape=jax.ShapeDtypeStruct((t, D_MODEL), F32),
        compiler_params=_params(("arbitrary",)),
        name="ffn_down",
    )(act, w_out_bf, x1, gfin)


def kernel(x_prompt, x_sample, cache_k, cache_v, state_conv, state_ffn, page_table, norm_mix_g, w_in, conv_w,
           conv_b, conv_ln_g, conv_ln_b, w_conv_out, lambda_q1, lambda_k1, lambda_q2, lambda_k2, subln_g,
           w_attn_out, w_o, norm_ffn_g, w_ffn_in, ffn_conv_w, ffn_conv_b, w_ffn_out, norm_final_g):
    batch, seq, _ = x_prompt.shape
    nb, dec, _ = x_sample.shape
    n_pool = cache_k.shape[1]
    past_len = page_table.shape[1] * PAGE_SIZE
    width = N_HEADS * V_DIM
    layer = 0

    row = lambda a: a[layer].reshape(1, -1)
    w_in_bf = w_in[layer].astype(BF16)
    g_mix, g_ffn, g_fin = row(norm_mix_g), row(norm_ffn_g), norm_final_g.reshape(1, -1)
    cw, cb, lng, lnb = conv_w[layer], row(conv_b), row(conv_ln_g), row(conv_ln_b)
    fcw, fcb = ffn_conv_w[layer], row(ffn_conv_b)
    lams = (row(lambda_q1), row(lambda_k1), row(lambda_q2), row(lambda_k2))
    sub_g = row(subln_g)
    slopes = jnp.asarray([2.0 ** -(h + 1) for h in range(N_HEADS)], F32)

    xp = x_prompt.reshape(batch * seq, D_MODEL)
    u_p, (q_p, kt_p, vf_p), gates_p, (wfo_bf, wfi_bf, wco_bf, wao_bf, wo_bf) = _in_proj(
        xp, g_mix, w_in_bf, 512, 1024, prompt_seq=seq, casts=(w_ffn_out[layer], w_ffn_in[layer]),
        qkv_casts=(w_conv_out[layer], w_attn_out[layer], w_o[layer]))
    c_p = _conv_prompt(u_p, cw, cb, lng, lnb, seq, 512)
    od_p = _attn_prompt(q_p, kt_p, vf_p, slopes, lams, sub_g, batch, seq)
    x1_p, h2_p = _merge(c_p, od_p, gates_p, xp, wco_bf, wao_bf, wo_bf, g_ffn, 256)
    ffn_tm = 1024
    act_p, tail_p = _ffn_up_prompt(h2_p, wfi_bf, fcw, fcb, seq, ffn_tm)
    y_p = _ffn_down(act_p, x1_p, wfo_bf, g_fin, 256)

    xs = x_sample.reshape(nb * dec, D_MODEL)
    u_s, qkv_s, gates_s, _ = _in_proj(xs, g_mix, w_in_bf, nb * dec, nb * dec)
    c_s, conv_s = _conv_sample(u_s.reshape(nb, dec, D_CONV), state_conv[layer], cw, cb, lng, lnb)
    cache_kt = jnp.transpose(cache_k[layer], (0, 2, 3, 4, 1)).reshape(n_pool, width, PAGE_SIZE)
    cache_vf = cache_v[layer].reshape(n_pool, PAGE_SIZE * N_HEADS, V_DIM)
    qkv_s3 = qkv_s.reshape(nb, dec, 3 * width)
    od_s = _attn_sample(qkv_s3, qkv_s3, qkv_s3, (0, 1, 2), cache_kt, cache_vf, page_table, lams, sub_g, past_len)
    x1_s, h2_s = _merge(c_s.reshape(nb * dec, D_CONV), od_s.reshape(nb * dec, width), gates_s, xs,
                        wco_bf, wao_bf, wo_bf, g_ffn, nb * dec)
    st = state_ffn[layer]
    pad = lambda a: jnp.pad(a, ((0, 0), (0, dec - a.shape[1]), (0, 0))).reshape(nb * dec, D_FF)
    e0 = pad(st)
    e1 = pad(st[:, 1:2])
    act_s, gate_s = _ffn_up_sample(h2_s, e0, e1, wfi_bf, fcw, fcb, dec)
    y_s = _ffn_down(act_s, x1_s, wfo_bf, g_fin, nb * dec)

    tiles_per_seq = seq // ffn_tm
    ffn_p = tail_p.reshape(batch, tiles_per_seq, 8, D_FF)[:, -1, 8 - (FFN_CONV_WIDTH - 1):, :]
    return (
        y_p.reshape(batch, seq, D_MODEL),
        y_s.reshape(nb, dec, D_MODEL),
        jnp.transpose(kt_p.reshape(1, batch, N_HEADS, 2, HEAD_DIM, seq), (0, 1, 5, 2, 3, 4)),
        vf_p.reshape(1, batch, seq, N_HEADS, V_DIM),
        u_p.reshape(batch, seq, D_CONV)[None, :, seq - (CONV_WIDTH - 1):, :],
        ffn_p[None],
        qkv_s[:, width:2 * width].reshape(1, nb, dec, N_HEADS, 2, HEAD_DIM),
        qkv_s[:, 2 * width:].reshape(1, nb, dec, N_HEADS, V_DIM),
        conv_s[None],
        gate_s.reshape(nb, dec, D_FF)[None, :, dec - (FFN_CONV_WIDTH - 1):, :],
    )
```
